```python
import math
import jax
import jax.numpy as jnp
from jax import lax
import numpy as np

D_MODEL = 2048
BATCH = 16
SEQ = 256
DEPTH = 4
DEC_BATCH = 8
DEC_SEQ = 1024
PAST_LEN = 512

GRID_W = 64
HEAD_DIM = 64
ROPE_BASE = 10000.0
EPS = 1e-6
QBLOCK = 128

NA_HEADS = 8
WIN_R = 8
WIN_C = 16
DF_HEADS = 4
MLA_HEADS = 4
Q_RANK = 384
KV_RANK = 128
NOPE_DIM = 128
ROPE_DIM = 64
V_DIM = 128
SSD_HEADS = 8
SSD_HEAD_DIM = 64
SSD_GROUPS = 2
D_STATE = 128
D_CONV = 4
CHUNK = 128
N_EXPERTS = 32
TOP_K = 4
D_FF = 2048
SWIGLU_ALPHA = 1.702
SWIGLU_LIMIT = 7.0
MOE_BLOCK = 256

N_BRANCH = 4
BRANCH_W = 512
NA_W = NA_HEADS * HEAD_DIM
DF_W = DF_HEADS * 2 * HEAD_DIM
MLA_QK = NOPE_DIM + ROPE_DIM
MLA_W = MLA_HEADS * V_DIM
SSD_W = SSD_HEADS * SSD_HEAD_DIM
SSD_BC = SSD_GROUPS * D_STATE
SSD_CONV_W = SSD_W + 2 * SSD_BC
IN_SPLITS = (NA_W, NA_W, NA_W, DF_W, DF_W, DF_W, Q_RANK, KV_RANK + ROPE_DIM, SSD_W, SSD_CONV_W, 2 * SSD_HEADS, N_BRANCH * D_MODEL)
IN_W = 3 * NA_W + 3 * DF_W + Q_RANK + KV_RANK + ROPE_DIM + SSD_W + SSD_CONV_W + 2 * SSD_HEADS + N_BRANCH * D_MODEL

kernel_name = 'hybrid_diffusion_trunk_step'

F32 = jnp.float32


def rmsnorm(x, g):
    xf = x.astype(F32)
    y = xf * lax.rsqrt(jnp.mean(xf * xf, axis=-1, keepdims=True) + EPS)
    return (y * g.astype(F32)).astype(x.dtype)


def rope_axis(x, pos):
    d = x.shape[-1]
    inv = jnp.asarray(ROPE_BASE ** (-np.arange(0, d, 2) / d), F32)
    ang = jnp.asarray(pos, F32)[:, None] * inv[None, :]
    shape = (pos.shape[0],) + (1,) * (x.ndim - 3) + (d // 2,)
    cos = jnp.cos(ang).reshape(shape)
    sin = jnp.sin(ang).reshape(shape)
    xf = x.astype(F32)
    x1, x2 = xf[..., : d // 2], xf[..., d // 2:]
    return jnp.concatenate([x1 * cos - x2 * sin, x2 * cos + x1 * sin], axis=-1).astype(x.dtype)


def rope_2d(x):
    t = np.arange(x.shape[1])
    half = x.shape[-1] // 2
    return jnp.concatenate([rope_axis(x[..., :half], t // GRID_W), rope_axis(x[..., half:], t % GRID_W)], axis=-1)


def attend(q, k, v, coef, scale):
    b, Tq, H, M, dq = q.shape
    qb = QBLOCK if Tq % QBLOCK == 0 else Tq
    nb = Tq // qb
    q_blocks = jnp.moveaxis(q.reshape(b, nb, qb, H, M, dq), 1, 0)

    def block(qx):
        s = jnp.einsum('bqhmd,bkhmd->bhmqk', qx, k).astype(F32) * scale
        w = jnp.einsum('m,bhmqk->bhqk', coef, jax.nn.softmax(s, axis=-1))
        return jnp.einsum('bhqk,bkhd->bqhd', w.astype(v.dtype), v)

    out = lax.map(block, q_blocks)
    return jnp.moveaxis(out, 0, 1).reshape(b, Tq, H, v.shape[-1])


def na_latent(q, k, v, k_ctx, v_ctx, rpb, scale):
    b, T, H, d = q.shape
    rows = T // GRID_W
    kr = min(WIN_R, rows)
    ncb = GRID_W // WIN_C
    kcb = 2 * WIN_C
    r = np.arange(rows)
    row_idx = np.clip(r - WIN_R // 2, 0, rows - kr)[:, None] + np.arange(kr)[None, :]
    col_idx = np.clip(np.arange(ncb) * WIN_C - WIN_C // 2, 0, GRID_W - kcb)[:, None] + np.arange(kcb)[None, :]
    qcol = np.arange(GRID_W).reshape(ncb, WIN_C)
    qstart = np.clip(qcol - WIN_C // 2, 0, GRID_W - WIN_C)[:, :, None]
    kcol = col_idx[:, None, :]
    valid = (kcol >= qstart) & (kcol < qstart + WIN_C)
    r_off = row_idx - r[:, None] + WIN_R - 1
    c_off = np.clip(kcol - qcol[:, :, None] + WIN_C - 1, 0, 2 * WIN_C - 2)
    bias = rpb.astype(F32)[:, r_off[:, None, None, :, None], c_off[None, :, :, None, :]]
    bias = jnp.where(valid[None, None, :, :, None, :], bias, -jnp.inf).reshape(H, rows, ncb, WIN_C, kr * kcb)
    ri = row_idx[:, None, :, None]
    ci = col_idx[None, :, None, :]
    kg = k.reshape(b, rows, GRID_W, H, d)[:, ri, ci].reshape(b, rows, ncb, kr * kcb, H, d)
    vg = v.reshape(b, rows, GRID_W, H, d)[:, ri, ci].reshape(b, rows, ncb, kr * kcb, H, d)
    qg = q.reshape(b, rows, ncb, WIN_C, H, d)
    s_loc = jnp.einsum('brcqhd,brckhd->bhrcqk', qg, kg).astype(F32) * scale + bias[None]
    s_ctx = jnp.einsum('brcqhd,bshd->bhrcqs', qg, k_ctx).astype(F32) * scale
    pr = jax.nn.softmax(jnp.concatenate([s_loc, s_ctx], axis=-1), axis=-1).astype(v.dtype)
    n_loc = kr * kcb
    out = jnp.einsum('bhrcqk,brckhd->brcqhd', pr[..., :n_loc], vg) + jnp.einsum('bhrcqs,bshd->brcqhd', pr[..., n_loc:], v_ctx)
    return out.reshape(b, T, H, d)


def mla_keys_values(ckv, krope, w_ukv, g_k):
    b, T, _ = ckv.shape
    kv = (ckv @ w_ukv).reshape(b, T, MLA_HEADS, NOPE_DIM + V_DIM)
    k = jnp.concatenate([kv[..., :NOPE_DIM], jnp.broadcast_to(krope[:, :, None, :], (b, T, MLA_HEADS, ROPE_DIM))], axis=-1)
    return rmsnorm(k, g_k), kv[..., NOPE_DIM:]


def ssd_chunked(x, dt, a, bm, cm, s0):
    b, T, H, P = x.shape
    N = bm.shape[-1]
    L = CHUNK if T % CHUNK == 0 else T
    nc = T // L
    xf = x.astype(F32).reshape(b, nc, L, H, P)
    bf = bm.astype(F32).reshape(b, nc, L, H, N)
    cf = cm.astype(F32).reshape(b, nc, L, H, N)
    dtc = dt.reshape(b, nc, L, H)
    a_cs = jnp.cumsum(dtc * a, axis=2)
    xdt = xf * dtc[..., None]
    seg = a_cs[:, :, :, None, :] - a_cs[:, :, None, :, :]
    tril = np.tril(np.ones((L, L), dtype=bool))[None, None, :, :, None]
    decay = jnp.exp(jnp.where(tril, seg, -jnp.inf))
    cb = jnp.einsum('bclhn,bcshn->bclsh', cf, bf) * decay
    y_diag = jnp.einsum('bclsh,bcshp->bclhp', cb, xdt)
    to_end = jnp.exp(a_cs[:, :, -1:, :] - a_cs)
    chunk_states = jnp.einsum('bclhn,bclh,bclhp->bchpn', bf, to_end, xdt)
    chunk_decay = jnp.exp(a_cs[:, :, -1, :])

    def step(s, inp):
        dec, st = inp
        return s * dec[:, :, None, None] + st, s

    s_fin, s_start = lax.scan(step, s0.astype(F32), (jnp.moveaxis(chunk_decay, 1, 0), jnp.moveaxis(chunk_states, 1, 0)))
    s_start = jnp.moveaxis(s_start, 0, 1)
    y_off = jnp.einsum('bclhn,bchpn,bclh->bclhp', cf, s_start, jnp.exp(a_cs))
    return (y_diag + y_off).reshape(b, T, H, P), s_fin


def ssd_mixer(z, xbc, dt_raw, p, s0f, s0b):
    b, T, _ = z.shape
    u = lax.conv_general_dilated(xbc, p['conv_w'][:, None, :], (1,), [((D_CONV - 1) // 2, D_CONV // 2)],
                                 dimension_numbers=('NWC', 'WIO', 'NWC'), feature_group_count=SSD_CONV_W)
    u = jax.nn.silu(u + p['conv_b'])
    xs, bm, cm = jnp.split(u, [SSD_W, SSD_W + SSD_BC], axis=-1)
    rep = SSD_HEADS // SSD_GROUPS
    xh = xs.reshape(b, T, SSD_HEADS, SSD_HEAD_DIM)
    bh = jnp.repeat(bm.reshape(b, T, SSD_GROUPS, D_STATE), rep, axis=2)
    ch = jnp.repeat(cm.reshape(b, T, SSD_GROUPS, D_STATE), rep, axis=2)
    dt = jax.nn.softplus(dt_raw.astype(F32).reshape(b, T, 2, SSD_HEADS) + p['dt_bias'].astype(F32))
    a = -jnp.exp(p['a_log'].astype(F32))
    yf, sf = ssd_chunked(xh, dt[:, :, 0], a[0], bh, ch, s0f)
    yb, sb = ssd_chunked(xh[:, ::-1], dt[:, ::-1, 1], a[1], bh[:, ::-1], ch[:, ::-1], s0b)
    y = yf + yb[:, ::-1] + p['d_skip'].astype(F32)[:, None] * xh.astype(F32)
    y = y.reshape(b, T, SSD_W) * jax.nn.silu(z.astype(F32))
    return rmsnorm(y, p['g_ssd']).astype(z.dtype), jnp.stack([sf, sb], axis=1)


def token_mixers(h, p, lam_init, ctx):
    b, T, _ = h.shape
    latent = ctx is not None
    parts = jnp.split(h @ p['w_in'], np.cumsum(IN_SPLITS)[:-1].tolist(), axis=-1)
    qa, ka, va, qb, kb, vb, dq, dkv, z, xbc, dt_raw, gate_in = parts
    ones1 = jnp.ones((1,), F32)

    qa = rmsnorm(qa.reshape(b, T, NA_HEADS, HEAD_DIM), p['g_na_q'])
    ka = rmsnorm(ka.reshape(b, T, NA_HEADS, HEAD_DIM), p['g_na_k'])
    va = va.reshape(b, T, NA_HEADS, HEAD_DIM)
    if latent:
        ya = na_latent(qa, ka, va, ctx['na_k'], ctx['na_v'], p['rpb'], HEAD_DIM ** -0.5)
    else:
        ya = attend(qa[:, :, :, None], ka[:, :, :, None], va, ones1, HEAD_DIM ** -0.5)

    qb = rmsnorm(qb.reshape(b, T, DF_HEADS, 2, HEAD_DIM), p['g_df_q'])
    kb = rmsnorm(kb.reshape(b, T, DF_HEADS, 2, HEAD_DIM), p['g_df_k'])
    vb = vb.reshape(b, T, DF_HEADS, 2 * HEAD_DIM)
    lam = (jnp.exp(jnp.sum(p['lam_q1'].astype(F32) * p['lam_k1'].astype(F32)))
           - jnp.exp(jnp.sum(p['lam_q2'].astype(F32) * p['lam_k2'].astype(F32))) + lam_init)
    coef = jnp.stack([jnp.ones((), F32), -lam])
    if latent:
        qb_in = rope_2d(qb)
        kb_all = jnp.concatenate([rope_2d(kb), ctx['df_k']], axis=1)
        vb_all = jnp.concatenate([vb, ctx['df_v']], axis=1)
    else:
        qb_in, kb_all, vb_all = qb, kb, vb
    yb = attend(qb_in, kb_all, vb_all, coef, HEAD_DIM ** -0.5)
    yb = rmsnorm(yb, p['g_df_sub']) * (1.0 - lam_init)

    cq = rmsnorm(dq, p['g_mla_cq'])
    qc = rmsnorm((cq @ p['w_uq']).reshape(b, T, MLA_HEADS, MLA_QK), p['g_mla_q'])
    ckv = rmsnorm(dkv[..., :KV_RANK], p['g_mla_ckv'])
    krope = dkv[..., KV_RANK:]
    kc, vc = mla_keys_values(ckv, krope, p['w_ukv'], p['g_mla_k'])
    if latent:
        qc = jnp.concatenate([qc[..., :NOPE_DIM], rope_2d(qc[..., NOPE_DIM:])], axis=-1)
        kc = jnp.concatenate([kc[..., :NOPE_DIM], rope_2d(kc[..., NOPE_DIM:])], axis=-1)
        kcx, vcx = mla_keys_values(ctx['mla_ckv'], ctx['mla_krope'], p['w_ukv'], p['g_mla_k'])
        kc = jnp.concatenate([kc, kcx], axis=1)
        vc = jnp.concatenate([vc, vcx], axis=1)
    yc = attend(qc[:, :, :, None], kc[:, :, :, None], vc, ones1, MLA_QK ** -0.5)

    if latent:
        s0f, s0b = ctx['ssd'][:, 0], ctx['ssd'][:, 1]
    else:
        s0f = jnp.zeros((b, SSD_HEADS, SSD_HEAD_DIM, D_STATE), F32)
        s0b = s0f
    yd, ssd_state = ssd_mixer(z, xbc, dt_raw, p, s0f, s0b)

    gates = jax.nn.sigmoid(gate_in.astype(F32)).reshape(b, T, N_BRANCH, D_MODEL)
    merged = jnp.zeros((b, T, D_MODEL), F32)
    for i, y in enumerate((ya, yb, yc, yd)):
        merged = merged + gates[:, :, i] * (y.reshape(b, T, BRANCH_W) @ p['w_branch'][i]).astype(F32)
    out = merged.astype(h.dtype) @ p['w_o']
    if latent:
        return out, None
    return out, dict(na_k=ka, na_v=va, df_k=kb, df_v=vb, mla_ckv=ckv, mla_krope=krope, ssd=ssd_state.astype(h.dtype))


def moe(h, w_router, b_router, w_gu, b_gu, w_dn, b_dn):
    N, D = h.shape
    logits = (h @ w_router).astype(F32) + b_router.astype(F32)
    top_v, top_i = lax.top_k(logits, TOP_K)
    gate_w = jax.nn.softmax(top_v, axis=-1)
    A = N * TOP_K
    flat_e = top_i.reshape(-1)
    flat_t = jnp.repeat(jnp.arange(N, dtype=jnp.int32), TOP_K)
    order = jnp.argsort(flat_e)
    se, st, sw = flat_e[order], flat_t[order], gate_w.reshape(-1)[order]
    counts = jnp.bincount(flat_e, length=N_EXPERTS)
    starts = jnp.cumsum(counts) - counts
    padded = (counts + MOE_BLOCK - 1) // MOE_BLOCK * MOE_BLOCK
    pad_ends = jnp.cumsum(padded)
    pad_starts = pad_ends - padded
    dest = pad_starts[se] + jnp.arange(A, dtype=jnp.int32) - starts[se]
    nblk = -(-A // MOE_BLOCK) + N_EXPERTS
    cap = nblk * MOE_BLOCK
    buf_t = jnp.full((cap,), N, jnp.int32).at[dest].set(st)
    buf_w = jnp.zeros((cap,), F32).at[dest].set(sw)
    blk_e = jnp.clip(jnp.searchsorted(pad_ends, jnp.arange(nblk, dtype=jnp.int32) * MOE_BLOCK, side='right'), 0, N_EXPERTS - 1)
    hp = jnp.concatenate([h, jnp.zeros((1, D), h.dtype)], axis=0)

    def expert_block(args):
        e, t = args
        xb = hp[t]
        gu = xb @ w_gu[e] + b_gu[e]
        g = jnp.minimum(gu[..., :D_FF], SWIGLU_LIMIT)
        u = jnp.clip(gu[..., D_FF:], -SWIGLU_LIMIT, SWIGLU_LIMIT)
        act = g * jax.nn.sigmoid(SWIGLU_ALPHA * g) * (u + 1.0)
        return act @ w_dn[e] + b_dn[e]

    yb = lax.map(expert_block, (blk_e, buf_t.reshape(nblk, MOE_BLOCK)))
    y = jnp.zeros((N + 1, D), F32).at[buf_t].add(yb.reshape(cap, D).astype(F32) * buf_w[:, None])
    return y[:N].astype(h.dtype)


def setup_inputs(seed: int = 0) -> dict:
    key = jax.random.key(seed)
    ks = iter(jax.random.split(key, 64))
    D = D_MODEL

    def nrm(shape, s=1.0):
        return jax.random.normal(next(ks), shape, F32) * s

    def gain(shape):
        return 1.0 + nrm(shape, 0.01)

    u_dt = jax.random.uniform(next(ks), (DEPTH, 2, SSD_HEADS), F32)
    dt0 = jnp.exp(u_dt * (math.log(0.1) - math.log(0.001)) + math.log(0.001))
    return {
        'x_prompt': nrm((BATCH, SEQ, D)),
        'x_sample': nrm((DEC_BATCH, DEC_SEQ, D)),
        'cache_na_k': nrm((DEC_BATCH, DEPTH, PAST_LEN, NA_HEADS, HEAD_DIM)),
        'cache_na_v': nrm((DEC_BATCH, DEPTH, PAST_LEN, NA_HEADS, HEAD_DIM)),
        'cache_df_k': nrm((DEC_BATCH, DEPTH, PAST_LEN, DF_HEADS, 2, HEAD_DIM)),
        'cache_df_v': nrm((DEC_BATCH, DEPTH, PAST_LEN, DF_HEADS, 2 * HEAD_DIM)),
        'cache_mla_ckv': nrm((DEC_BATCH, DEPTH, PAST_LEN, KV_RANK)),
        'cache_mla_krope': nrm((DEC_BATCH, DEPTH, PAST_LEN, ROPE_DIM)),
        'state_ssd': nrm((DEC_BATCH, DEPTH, 2, SSD_HEADS, SSD_HEAD_DIM, D_STATE), 0.5),
        'c': nrm((DEC_BATCH, D)),
        'c_ctx': nrm((D,)),
        'w_mod': nrm((DEPTH, D, 6 * D), 0.5 * D ** -0.5),
        'b_mod': nrm((DEPTH, 6 * D), 0.01),
        'g_norm1': gain((DEPTH, D)),
        'g_norm2': gain((DEPTH, D)),
        'w_in': nrm((DEPTH, D, IN_W), D ** -0.5),
        'g_na_q': gain((DEPTH, HEAD_DIM)),
        'g_na_k': gain((DEPTH, HEAD_DIM)),
        'rpb': nrm((DEPTH, NA_HEADS, 2 * WIN_R - 1, 2 * WIN_C - 1), 0.1),
        'g_df_q': gain((DEPTH, HEAD_DIM)),
        'g_df_k': gain((DEPTH, HEAD_DIM)),
        'lam_q1': nrm((DEPTH, HEAD_DIM), 0.1),
        'lam_k1': nrm((DEPTH, HEAD_DIM), 0.1),
        'lam_q2': nrm((DEPTH, HEAD_DIM), 0.1),
        'lam_k2': nrm((DEPTH, HEAD_DIM), 0.1),
        'g_df_sub': gain((DEPTH, 2 * HEAD_DIM)),
        'g_mla_cq': gain((DEPTH, Q_RANK)),
        'g_mla_ckv': gain((DEPTH, KV_RANK)),
        'w_uq': nrm((DEPTH, Q_RANK, MLA_HEADS * MLA_QK), Q_RANK ** -0.5),
        'w_ukv': nrm((DEPTH, KV_RANK, MLA_HEADS * (NOPE_DIM + V_DIM)), KV_RANK ** -0.5),
        'g_mla_q': gain((DEPTH, MLA_QK)),
        'g_mla_k': gain((DEPTH, MLA_QK)),
        'conv_w': nrm((DEPTH, D_CONV, SSD_CONV_W), D_CONV ** -0.5),
        'conv_b': nrm((DEPTH, SSD_CONV_W), 0.01),
        'dt_bias': dt0 + jnp.log(-jnp.expm1(-dt0)),
        'a_log': jnp.log(jax.random.uniform(next(ks), (DEPTH, 2, SSD_HEADS), F32, 1.0, 16.0)),
        'd_skip': gain((DEPTH, SSD_HEADS)),
        'g_ssd': gain((DEPTH, SSD_W)),
        'w_branch': nrm((DEPTH, N_BRANCH, BRANCH_W, D), BRANCH_W ** -0.5),
        'w_o': nrm((DEPTH, D, D), D ** -0.5),
        'w_router': nrm((DEPTH, D, N_EXPERTS), D ** -0.5),
        'b_router': nrm((DEPTH, N_EXPERTS), 0.01),
        'w_gu': nrm((DEPTH, N_EXPERTS, D, 2 * D_FF), D ** -0.5),
        'b_gu': nrm((DEPTH, N_EXPERTS, 2 * D_FF), 0.01),
        'w_dn': nrm((DEPTH, N_EXPERTS, D_FF, D), D_FF ** -0.5),
        'b_dn': nrm((DEPTH, N_EXPERTS, D), 0.01),
    }


def reference(x_prompt, x_sample, cache_na_k, cache_na_v, cache_df_k, cache_df_v, cache_mla_ckv, cache_mla_krope,
              state_ssd, c, c_ctx, w_mod, b_mod, g_norm1, g_norm2, w_in, g_na_q, g_na_k, rpb, g_df_q, g_df_k,
              lam_q1, lam_k1, lam_q2, lam_k2, g_df_sub, g_mla_cq, g_mla_ckv, w_uq, w_ukv, g_mla_q, g_mla_k,
              conv_w, conv_b, dt_bias, a_log, d_skip, g_ssd, w_branch, w_o, w_router, b_router, w_gu, b_gu,
              w_dn, b_dn):
    xp, xs = x_prompt, x_sample
    n_p = xp.shape[0] * xp.shape[1]
    sc_c = jax.nn.silu(c)
    sc_x = jax.nn.silu(c_ctx)
    names = ('na_k', 'na_v', 'df_k', 'df_v', 'mla_ckv', 'mla_krope', 'ssd')
    new = {n: [] for n in names}
    for l in range(DEPTH):
        lam_init = 0.8 - 0.6 * math.exp(-0.3 * l)
        p = dict(w_in=w_in[l], g_na_q=g_na_q[l], g_na_k=g_na_k[l], rpb=rpb[l], g_df_q=g_df_q[l], g_df_k=g_df_k[l],
                 lam_q1=lam_q1[l], lam_k1=lam_k1[l], lam_q2=lam_q2[l], lam_k2=lam_k2[l], g_df_sub=g_df_sub[l],
                 g_mla_cq=g_mla_cq[l], g_mla_ckv=g_mla_ckv[l], w_uq=w_uq[l], w_ukv=w_ukv[l], g_mla_q=g_mla_q[l],
                 g_mla_k=g_mla_k[l], conv_w=conv_w[l], conv_b=conv_b[l], dt_bias=dt_bias[l], a_log=a_log[l],
                 d_skip=d_skip[l], g_ssd=g_ssd[l], w_branch=w_branch[l], w_o=w_o[l])
        mc = [m[:, None, :] for m in jnp.split(sc_c @ w_mod[l] + b_mod[l], 6, axis=-1)]
        mx = jnp.split(sc_x @ w_mod[l] + b_mod[l], 6, axis=-1)
        hp = rmsnorm(xp, g_norm1[l]) * (1.0 + mx[1]) + mx[0]
        op, ctx_new = token_mixers(hp, p, lam_init, None)
        xp = xp + mx[2] * op
        ctx_l = dict(na_k=cache_na_k[:, l], na_v=cache_na_v[:, l], df_k=cache_df_k[:, l], df_v=cache_df_v[:, l],
                     mla_ckv=cache_mla_ckv[:, l], mla_krope=cache_mla_krope[:, l], ssd=state_ssd[:, l])
        hs = rmsnorm(xs, g_norm1[l]) * (1.0 + mc[1]) + mc[0]
        os_, _ = token_mixers(hs, p, lam_init, ctx_l)
        xs = xs + mc[2] * os_
        hp2 = rmsnorm(xp, g_norm2[l]) * (1.0 + mx[4]) + mx[3]
        hs2 = rmsnorm(xs, g_norm2[l]) * (1.0 + mc[4]) + mc[3]
        m = moe(jnp.concatenate([hp2.reshape(-1, D_MODEL), hs2.reshape(-1, D_MODEL)], axis=0),
                w_router[l], b_router[l], w_gu[l], b_gu[l], w_dn[l], b_dn[l])
        xp = xp + mx[5] * m[:n_p].reshape(xp.shape)
        xs = xs + mc[5] * m[n_p:].reshape(xs.shape)
        for n in names:
            new[n].append(ctx_new[n])
    return (xp, xs, jnp.stack(new['na_k'], axis=1), jnp.stack(new['na_v'], axis=1), jnp.stack(new['df_k'], axis=1),
            jnp.stack(new['df_v'], axis=1), jnp.stack(new['mla_ckv'], axis=1), jnp.stack(new['mla_krope'], axis=1),
            jnp.stack(new['ssd'], axis=1))
```

```python
import functools
import math

import jax
import jax.numpy as jnp
from jax import lax
import numpy as np
from jax.experimental import pallas as pl
from jax.experimental.pallas import tpu as pltpu

F32 = jnp.float32
BF16 = jnp.bfloat16

GRID_W = 64
HEAD_DIM = 64
ROPE_BASE = 10000.0
EPS = 1e-6
NA_HEADS = 8
WIN_R = 8
WIN_C = 16
DF_HEADS = 4
MLA_HEADS = 4
Q_RANK = 384
KV_RANK = 128
NOPE_DIM = 128
ROPE_DIM = 64
V_DIM = 128
MLA_QK = NOPE_DIM + ROPE_DIM
SSD_HEADS = 8
SSD_HEAD_DIM = 64
SSD_GROUPS = 2
D_STATE = 128
CHUNK = 128
TOP_K = 4
SWIGLU_ALPHA = 1.702
SWIGLU_LIMIT = 7.0
N_BRANCH = 4
BRANCH_W = 512

V7X_VMEM_BYTES = 64 * 1024 * 1024
VMEM_LIMIT = V7X_VMEM_BYTES - 8 * 1024 * 1024
LANE = 128

MOE_TOKENS = 512
MOE_FF_CHUNK = 512


def _cparams(*sem):
    return pltpu.CompilerParams(dimension_semantics=sem, vmem_limit_bytes=VMEM_LIMIT)


def _tile(n, pref):
    t = min(n, pref)
    while n % t:
        t //= 2
    return t


def _dot(a, b):
    return jnp.dot(a, b, preferred_element_type=F32)


def _dot_nt(a, b):
    return lax.dot_general(a, b, (((1,), (1,)), ((), ())), preferred_element_type=F32)


def _dot_tn(a, b):
    return lax.dot_general(a, b, (((0,), (0,)), ((), ())), preferred_element_type=F32)


def _split3(x):
    h = x.astype(BF16)
    r = x - h.astype(F32)
    m = r.astype(BF16)
    lo = (r - m.astype(F32)).astype(BF16)
    return h, m, lo


def _sel_dot(x, sel):
    h, m, lo = _split3(x)
    return _dot(h, sel) + _dot(m, sel) + _dot(lo, sel)


def _dot_hi(a, b):
    a1, a2, a3 = _split3(a)
    b1, b2, b3 = _split3(b)
    return (_dot(a1, b1) + (_dot(a1, b2) + _dot(a2, b1))
            + (_dot(a1, b3) + _dot(a2, b2) + _dot(a3, b1)))


def _rope_rotate(y, cos, sin):
    w = y.shape[-1]
    lane = lax.broadcasted_iota(jnp.int32, y.shape, 1)
    partner = jnp.where((lane % 32) < 16, pltpu.roll(y, w - 16, 1), pltpu.roll(y, 16, 1))
    return y * cos + partner * sin


def _rope_tables(t_len, width):
    half = HEAD_DIM // 2
    inv = jnp.asarray(ROPE_BASE ** (-np.arange(0, half, 2) / half), F32)
    t = np.arange(t_len)
    cols = []
    sins = []
    for pos in (t // GRID_W, t % GRID_W):
        ang = jnp.asarray(pos, F32)[:, None] * inv[None, :]
        c, s = jnp.cos(ang), jnp.sin(ang)
        cols += [c, c]
        sins += [-s, s]
    cos = jnp.concatenate(cols, axis=-1)
    sin = jnp.concatenate(sins, axis=-1)
    rep = width // HEAD_DIM
    return jnp.tile(cos, (1, rep)), jnp.tile(sin, (1, rep))


def _group_selector(width, group):
    g = np.arange(width) // group
    return jnp.asarray(g[:, None] == g[None, :], BF16)


def _mod_kernel(c_ref, w_ref, b_ref, o_ref):
    c = c_ref[...]
    s = (c * jax.nn.sigmoid(c)).astype(BF16)
    o_ref[0] = _dot(s, w_ref[0].astype(BF16)) + b_ref[0]


def _mod_params(cc, w_mod, b_mod):
    depth, d, n = w_mod.shape
    rows = cc.shape[0]
    tn = _tile(n, 1024)
    return pl.pallas_call(
        _mod_kernel,
        grid=(depth, n // tn),
        in_specs=[pl.BlockSpec((rows, d), lambda l, j: (0, 0)),
                  pl.BlockSpec((1, d, tn), lambda l, j: (l, 0, j)),
                  pl.BlockSpec((1, 1, tn), lambda l, j: (l, 0, j))],
        out_specs=pl.BlockSpec((1, rows, tn), lambda l, j: (l, 0, j)),
        out_shape=jax.ShapeDtypeStruct((depth, rows, n), F32),
        compiler_params=_cparams("parallel", "parallel"),
    )(cc, w_mod, b_mod.reshape(depth, 1, n))


def _norm_mod_body(x_ref, g_ref, sh_ref, sc_ref):
    x = x_ref[0]
    y = x * lax.rsqrt(jnp.mean(x * x, axis=-1, keepdims=True) + EPS)
    y = y * g_ref[...]
    return y * (1.0 + sc_ref[0]) + sh_ref[0]


def _norm_mod_kernel(x_ref, g_ref, sh_ref, sc_ref, o_ref):
    o_ref[0] = _norm_mod_body(x_ref, g_ref, sh_ref, sc_ref).astype(o_ref.dtype)


def _norm_mod_router_kernel(x_ref, g_ref, sh_ref, sc_ref, wr_ref, br_ref, o_ref, lg_ref):
    y = _norm_mod_body(x_ref, g_ref, sh_ref, sc_ref)
    o_ref[0] = y.astype(o_ref.dtype)
    lg_ref[0] = _dot_hi(y, wr_ref[...]) + br_ref[...]


def _mod_spec(m, d):
    if m.shape[0] == 1:
        return pl.BlockSpec((1, 1, d), lambda b, t: (0, 0, 0))
    return pl.BlockSpec((1, 1, d), lambda b, t: (b, 0, 0))


def _norm_mod(x, g, shift, scale, router=None):
    bsz, t_len, d = x.shape
    tt = _tile(t_len, 256)
    in_specs = [pl.BlockSpec((1, tt, d), lambda b, t: (b, t, 0)),
                pl.BlockSpec((1, d), lambda b, t: (0, 0)),
                _mod_spec(shift, d), _mod_spec(scale, d)]
    out_spec = pl.BlockSpec((1, tt, d), lambda b, t: (b, t, 0))
    out_shape = jax.ShapeDtypeStruct((bsz, t_len, d), BF16)
    args = [x, g.reshape(1, d), shift, scale]
    if router is None:
        return pl.pallas_call(
            _norm_mod_kernel, grid=(bsz, t_len // tt), in_specs=in_specs, out_specs=out_spec,
            out_shape=out_shape, compiler_params=_cparams("parallel", "parallel"))(*args)
    wr, br = router
    ne = wr.shape[1]
    in_specs += [pl.BlockSpec((d, ne), lambda b, t: (0, 0)), pl.BlockSpec((1, ne), lambda b, t: (0, 0))]
    return pl.pallas_call(
        _norm_mod_router_kernel, grid=(bsz, t_len // tt), in_specs=in_specs,
        out_specs=[out_spec, pl.BlockSpec((1, tt, ne), lambda b, t: (b, t, 0))],
        out_shape=[out_shape, jax.ShapeDtypeStruct((bsz, t_len, ne), F32)],
        compiler_params=_cparams("parallel", "parallel"))(*args, wr, br)


def _mm_kernel(x_ref, w_ref, o_ref):
    o_ref[0] = _dot(x_ref[0].astype(BF16), w_ref[...].astype(BF16)).astype(o_ref.dtype)


def _mm_res_kernel(x_ref, w_ref, r_ref, g_ref, o_ref):
    o_ref[0] = r_ref[0] + g_ref[0] * _dot(x_ref[0].astype(BF16), w_ref[...].astype(BF16))


def _w_spec(w, layer, k, tn):
    if w.ndim == 3:
        return pl.BlockSpec((None, k, tn), lambda n, b, t: (layer, 0, n))
    return pl.BlockSpec((k, tn), lambda n, b, t: (0, n))


def _mm(x, w, layer=0, out_dtype=F32, tm_pref=512, tn_pref=512):
    bsz, t_len, k = x.shape
    n = w.shape[-1]
    tm = _tile(t_len, tm_pref)
    tn = n if n % LANE else _tile(n, tn_pref)
    return pl.pallas_call(
        _mm_kernel,
        grid=(n // tn, bsz, t_len // tm),
        in_specs=[pl.BlockSpec((1, tm, k), lambda n_, b, t: (b, t, 0)), _w_spec(w, layer, k, tn)],
        out_specs=pl.BlockSpec((1, tm, tn), lambda n_, b, t: (b, t, n_)),
        out_shape=jax.ShapeDtypeStruct((bsz, t_len, n), out_dtype),
        compiler_params=_cparams("parallel", "parallel", "parallel"),
    )(x, w)


def _mm_residual(x, w, layer, res, gate):
    bsz, t_len, k = x.shape
    n = w.shape[-1]
    tm = _tile(t_len, 512)
    tn = _tile(n, 512)
    if gate.shape[0] == 1:
        g_spec = pl.BlockSpec((1, 1, tn), lambda n_, b, t: (0, 0, n_))
    else:
        g_spec = pl.BlockSpec((1, 1, tn), lambda n_, b, t: (b, 0, n_))
    return pl.pallas_call(
        _mm_res_kernel,
        grid=(n // tn, bsz, t_len // tm),
        in_specs=[pl.BlockSpec((1, tm, k), lambda n_, b, t: (b, t, 0)), _w_spec(w, layer, k, tn),
                  pl.BlockSpec((1, tm, tn), lambda n_, b, t: (b, t, n_)), g_spec],
        out_specs=pl.BlockSpec((1, tm, tn), lambda n_, b, t: (b, t, n_)),
        out_shape=jax.ShapeDtypeStruct((bsz, t_len, n), F32),
        compiler_params=_cparams("parallel", "parallel", "parallel"),
    )(x, w, res, gate)


def _headnorm_kernel(*refs, group, rope):
    if rope:
        x_ref, g_ref, e_ref, cos_ref, sin_ref, o_ref = refs
    else:
        x_ref, g_ref, e_ref, o_ref = refs
    x = x_ref[0]
    ms = _sel_dot(x * x, e_ref[...]) * (1.0 / group)
    y = x * lax.rsqrt(ms + EPS) * g_ref[...]
    if rope:
        y = _rope_rotate(y, cos_ref[...], sin_ref[...])
    o_ref[0] = y


def _headnorm(x, col, gain, group, rope=None):
    bsz, t_len, _ = x.shape
    w = BRANCH_W
    tt = _tile(t_len, 256)
    g = jnp.tile(gain, w // gain.shape[0]).reshape(1, w)
    in_specs = [pl.BlockSpec((1, tt, w), lambda b, t: (b, t, col)),
                pl.BlockSpec((1, w), lambda b, t: (0, 0)),
                pl.BlockSpec((w, w), lambda b, t: (0, 0))]
    args = [x, g, _group_selector(w, group)]
    if rope is not None:
        in_specs += [pl.BlockSpec((tt, w), lambda b, t: (t, 0))] * 2
        args += list(rope)
    return pl.pallas_call(
        functools.partial(_headnorm_kernel, group=group, rope=rope is not None),
        grid=(bsz, t_len // tt), in_specs=in_specs,
        out_specs=pl.BlockSpec((1, tt, w), lambda b, t: (b, t, 0)),
        out_shape=jax.ShapeDtypeStruct((bsz, t_len, w), F32),
        compiler_params=_cparams("parallel", "parallel"),
    )(*args)


def _attn_kernel(*refs, heads, maps, pieces, dv, scale, has_ctx, diff, post_scale):
    it = iter(refs)
    q_ref, k_ref, v_ref = next(it), next(it), next(it)
    kc_ref = vc_ref = None
    if has_ctx:
        kc_ref, vc_ref = next(it), next(it)
    if diff is not None:
        lq1, lk1, lq2, lk2, gsub_ref = next(it), next(it), next(it), next(it), next(it)
    o_ref = next(it)

    q = q_ref[0].astype(BF16)
    k = k_ref[0].astype(BF16)
    v = v_ref[0].astype(BF16)
    if has_ctx:
        kc = kc_ref[0, 0].astype(BF16)
        vc = vc_ref[0, 0].astype(BF16)
    if diff is not None:
        lam = (jnp.exp(jnp.sum(lq1[...] * lk1[...], axis=-1, keepdims=True))
               - jnp.exp(jnp.sum(lq2[...] * lk2[...], axis=-1, keepdims=True)) + diff)

    for h in range(heads):
        w_own = w_ctx = None
        for m in range(maps):
            s = s_c = None
            for (qo, ko, d) in pieces(h, m):
                part = _dot_nt(q[:, qo:qo + d], k[:, ko:ko + d])
                s = part if s is None else s + part
                if has_ctx:
                    part_c = _dot_nt(q[:, qo:qo + d], kc[:, ko:ko + d])
                    s_c = part_c if s_c is None else s_c + part_c
            s = s * scale
            mx = jnp.max(s, axis=-1, keepdims=True)
            if has_ctx:
                s_c = s_c * scale
                mx = jnp.maximum(mx, jnp.max(s_c, axis=-1, keepdims=True))
            p = jnp.exp(s - mx)
            den = jnp.sum(p, axis=-1, keepdims=True)
            if has_ctx:
                p_c = jnp.exp(s_c - mx)
                den = den + jnp.sum(p_c, axis=-1, keepdims=True)
            inv = 1.0 / den
            if m == 1:
                inv = -lam * inv
            w_own = p * inv if w_own is None else w_own + p * inv
            if has_ctx:
                w_ctx = p_c * inv if w_ctx is None else w_ctx + p_c * inv
        o = _dot(w_own.astype(BF16), v[:, h * dv:(h + 1) * dv])
        if has_ctx:
            o = o + _dot(w_ctx.astype(BF16), vc[:, h * dv:(h + 1) * dv])
        if diff is not None:
            o = o * lax.rsqrt(jnp.mean(o * o, axis=-1, keepdims=True) + EPS) * gsub_ref[...] * post_scale
        o_ref[0, :, h * dv:(h + 1) * dv] = o.astype(o_ref.dtype)


def _attention(q, k, v, v_col, *, heads, maps, pieces, dv, scale, ctx=None, diff=None):
    bsz, t_len, wq = q.shape
    wk = k.shape[-1]
    wv = heads * dv
    tq = _tile(t_len, 256)
    in_specs = [pl.BlockSpec((1, tq, wq), lambda b, t: (b, t, 0)),
                pl.BlockSpec((1, t_len, wk), lambda b, t: (b, 0, 0)),
                pl.BlockSpec((1, t_len, wv), lambda b, t: (b, 0, v_col))]
    args = [q, k, v]
    if ctx is not None:
        kc, vc, layer, vc_col = ctx
        tc = kc.shape[2]
        in_specs += [pl.BlockSpec((1, 1, tc, wk), lambda b, t: (b, layer, 0, 0)),
                     pl.BlockSpec((1, 1, tc, wv), lambda b, t: (b, layer, 0, vc_col))]
        args += [kc, vc]
    lam_init = 0.0
    if diff is not None:
        lq1, lk1, lq2, lk2, gsub, lam_init = diff
        for a in (lq1, lk1, lq2, lk2, gsub):
            in_specs.append(pl.BlockSpec((1, a.shape[0]), lambda b, t: (0, 0)))
            args.append(a.reshape(1, -1))
    kern = functools.partial(
        _attn_kernel, heads=heads, maps=maps, pieces=pieces, dv=dv, scale=scale,
        has_ctx=ctx is not None, diff=lam_init if diff is not None else None,
        post_scale=1.0 - lam_init)
    return pl.pallas_call(
        kern, grid=(bsz, t_len // tq), in_specs=in_specs,
        out_specs=pl.BlockSpec((1, tq, wv), lambda b, t: (b, t, 0)),
        out_shape=jax.ShapeDtypeStruct((bsz, t_len, wv), BF16),
        compiler_params=_cparams("parallel", "parallel"),
    )(*args)


def _na_pieces(h, m):
    return [(h * HEAD_DIM, h * HEAD_DIM, HEAD_DIM)]


def _df_pieces(h, m):
    o = (2 * h + m) * HEAD_DIM
    return [(o, o, HEAD_DIM)]


def _mla_pieces(h, m):
    nope_w = MLA_HEADS * NOPE_DIM
    return [(h * NOPE_DIM, h * NOPE_DIM, NOPE_DIM),
            (nope_w + h * ROPE_DIM, nope_w + h * ROPE_DIM, ROPE_DIM)]


def _na_latent_kernel(q_ref, k_ref, v_ref, kc_ref, vc_ref, bias_ref, o_ref, *, rows, kr, scale):
    r = pl.program_id(1)
    start = jnp.clip(r - WIN_R // 2, 0, rows - kr) * GRID_W
    start = pl.multiple_of(start, GRID_W)
    q = q_ref[0].astype(BF16)
    kl = k_ref[0, pl.ds(start, kr * GRID_W), :].astype(BF16)
    vl = v_ref[0, pl.ds(start, kr * GRID_W), :].astype(BF16)
    kc = kc_ref[0, 0].astype(BF16)
    vc = vc_ref[0, 0].astype(BF16)
    for h in range(NA_HEADS):
        sl = slice(h * HEAD_DIM, (h + 1) * HEAD_DIM)
        s_loc = _dot_nt(q[:, sl], kl[:, sl]) * scale + bias_ref[h, 0]
        s_ctx = _dot_nt(q[:, sl], kc[:, sl]) * scale
        mx = jnp.maximum(jnp.max(s_loc, axis=-1, keepdims=True), jnp.max(s_ctx, axis=-1, keepdims=True))
        p_loc = jnp.exp(s_loc - mx)
        p_ctx = jnp.exp(s_ctx - mx)
        inv = 1.0 / (jnp.sum(p_loc, axis=-1, keepdims=True) + jnp.sum(p_ctx, axis=-1, keepdims=True))
        o = _dot((p_loc * inv).astype(BF16), vl[:, sl]) + _dot((p_ctx * inv).astype(BF16), vc[:, sl])
        o_ref[0, :, sl] = o.astype(o_ref.dtype)


def _na_bias(rpb_l, rows):
    kr = min(WIN_R, rows)
    r = np.arange(rows)
    krow = np.clip(r - WIN_R // 2, 0, rows - kr)[:, None] + np.arange(kr)[None, :]
    r_off = krow - r[:, None] + WIN_R - 1
    qc = np.arange(GRID_W)
    kc = np.arange(GRID_W)
    qstart = np.clip(qc - WIN_C // 2, 0, GRID_W - WIN_C)
    valid = (kc[None, :] >= qstart[:, None]) & (kc[None, :] < qstart[:, None] + WIN_C)
    c_off = np.clip(kc[None, :] - qc[:, None] + WIN_C - 1, 0, 2 * WIN_C - 2)
    bias = rpb_l.astype(F32)[:, r_off[:, None, :, None], c_off[None, :, None, :]]
    bias = jnp.where(valid[None, None, :, None, :], bias, -jnp.inf)
    return bias.reshape(rpb_l.shape[0], rows, GRID_W, kr * GRID_W)


def _na_latent(q, k, v, v_col, kc, vc, layer, rpb_l):
    bsz, t_len, w = q.shape
    rows = t_len // GRID_W
    kr = min(WIN_R, rows)
    tc = kc.shape[2]
    bias = _na_bias(rpb_l, rows)
    return pl.pallas_call(
        functools.partial(_na_latent_kernel, rows=rows, kr=kr, scale=HEAD_DIM ** -0.5),
        grid=(bsz, rows),
        in_specs=[pl.BlockSpec((1, GRID_W, w), lambda b, r: (b, r, 0)),
                  pl.BlockSpec((1, t_len, w), lambda b, r: (b, 0, 0)),
                  pl.BlockSpec((1, t_len, w), lambda b, r: (b, 0, v_col)),
                  pl.BlockSpec((1, 1, tc, w), lambda b, r: (b, layer, 0, 0)),
                  pl.BlockSpec((1, 1, tc, w), lambda b, r: (b, layer, 0, 0)),
                  pl.BlockSpec((NA_HEADS, 1, GRID_W, kr * GRID_W), lambda b, r: (0, r, 0, 0))],
        out_specs=pl.BlockSpec((1, GRID_W, w), lambda b, r: (b, r, 0)),
        out_shape=jax.ShapeDtypeStruct((bsz, t_len, w), BF16),
        compiler_params=_cparams("parallel", "parallel"),
    )(q, k, v, kc, vc, bias)


def _mla_k_part(ckv_bf16, krope, wukv_ref, gk_ref, ek_ref, tile_ref, rope_tabs):
    nope_w = MLA_HEADS * NOPE_DIM
    kv = _dot(ckv_bf16, wukv_ref[...])
    k_nope = kv[:, :nope_w]
    kr_t = _sel_dot(krope, tile_ref[...])
    sq = jnp.concatenate([k_nope * k_nope, krope * krope], axis=-1)
    ms = _sel_dot(sq, ek_ref[...]) * (1.0 / MLA_QK)
    rs = lax.rsqrt(ms + EPS)
    g = gk_ref[...]
    kn = k_nope * rs[:, :nope_w] * g[:, :nope_w]
    kro = kr_t * rs[:, nope_w:] * g[:, nope_w:]
    if rope_tabs is not None:
        kro = _rope_rotate(kro, rope_tabs[0][...], rope_tabs[1][...])
    return jnp.concatenate([kn, kro], axis=-1), kv[:, nope_w:]


def _mla_prep_kernel(*refs, rope):
    it = iter(refs)
    x_ref = next(it)
    gcq_ref, gckv_ref, wuq_ref, wukv_ref, gq_ref, gk_ref, eq_ref, ek_ref, tile_ref = (next(it) for _ in range(9))
    tabs = (next(it), next(it)) if rope else None
    q_ref, k_ref, v_ref, ckv_ref, kr_ref = (next(it) for _ in range(5))
    nope_w = MLA_HEADS * NOPE_DIM

    x = x_ref[0]
    dq = x[:, :Q_RANK]
    cq = dq * lax.rsqrt(jnp.mean(dq * dq, axis=-1, keepdims=True) + EPS) * gcq_ref[...]
    dkv = x[:, Q_RANK:Q_RANK + KV_RANK]
    ckv = dkv * lax.rsqrt(jnp.mean(dkv * dkv, axis=-1, keepdims=True) + EPS) * gckv_ref[...]
    krope = x[:, Q_RANK + KV_RANK:]
    ckv_ref[0] = ckv
    kr_ref[0] = krope

    qr = _dot(cq.astype(BF16), wuq_ref[...])
    ms = _sel_dot(qr * qr, eq_ref[...]) * (1.0 / MLA_QK)
    qn = qr * lax.rsqrt(ms + EPS) * gq_ref[...]
    if rope:
        q_ref[0] = jnp.concatenate(
            [qn[:, :nope_w], _rope_rotate(qn[:, nope_w:], tabs[0][...], tabs[1][...])], axis=-1)
    else:
        q_ref[0] = qn
    kn, v = _mla_k_part(ckv.astype(BF16), krope, wukv_ref, gk_ref, ek_ref, tile_ref, tabs)
    k_ref[0] = kn
    v_ref[0] = v.astype(v_ref.dtype)


def _mla_ctx_kernel(ckv_ref, kr_ref, wukv_ref, gk_ref, ek_ref, tile_ref, k_ref, v_ref):
    kn, v = _mla_k_part(ckv_ref[0, 0].astype(BF16), kr_ref[0, 0], wukv_ref, gk_ref, ek_ref, tile_ref, None)
    k_ref[0, 0] = kn
    v_ref[0, 0] = v.astype(v_ref.dtype)


def _mla_consts():
    nope_w = MLA_HEADS * NOPE_DIM
    qk_w = nope_w + MLA_HEADS * ROPE_DIM
    head_q = np.concatenate([np.arange(nope_w) // NOPE_DIM, np.arange(MLA_HEADS * ROPE_DIM) // ROPE_DIM])
    eq = head_q[:, None] == head_q[None, :]
    ek = np.concatenate([eq[:nope_w], np.ones((ROPE_DIM, qk_w), bool)], axis=0)
    tile = np.arange(ROPE_DIM)[:, None] == (np.arange(MLA_HEADS * ROPE_DIM) % ROPE_DIM)[None, :]
    return jnp.asarray(eq, BF16), jnp.asarray(ek, BF16), jnp.asarray(tile, BF16)


def _mla_perm_weights(w_uq_l, w_ukv_l, g_q, g_k):
    wq = w_uq_l.reshape(Q_RANK, MLA_HEADS, MLA_QK)
    wq = jnp.concatenate([wq[:, :, :NOPE_DIM].reshape(Q_RANK, -1), wq[:, :, NOPE_DIM:].reshape(Q_RANK, -1)], axis=1)
    wkv = w_ukv_l.reshape(KV_RANK, MLA_HEADS, NOPE_DIM + V_DIM)
    wkv = jnp.concatenate([wkv[:, :, :NOPE_DIM].reshape(KV_RANK, -1), wkv[:, :, NOPE_DIM:].reshape(KV_RANK, -1)], axis=1)

    def gain(g):
        return jnp.concatenate([jnp.tile(g[:NOPE_DIM], MLA_HEADS), jnp.tile(g[NOPE_DIM:], MLA_HEADS)]).reshape(1, -1)

    return wq.astype(BF16), wkv.astype(BF16), gain(g_q), gain(g_k)


def _const_spec(a):
    nd = a.ndim
    return pl.BlockSpec(a.shape, lambda *_: (0,) * nd)


def _mla_prep(x, g_cq, g_ckv, wq, wkv, gq, gk, consts, rope):
    bsz, t_len, w = x.shape
    tt = _tile(t_len, 256)
    eq, ek, tile = consts
    qk_w = MLA_HEADS * MLA_QK
    args = [x, g_cq.reshape(1, -1), g_ckv.reshape(1, -1), wq, wkv, gq, gk, eq, ek, tile]
    in_specs = [pl.BlockSpec((1, tt, w), lambda b, t: (b, t, 0))] + [_const_spec(a) for a in args[1:]]
    if rope is not None:
        rw = MLA_HEADS * ROPE_DIM
        in_specs += [pl.BlockSpec((tt, rw), lambda b, t: (t, 0))] * 2
        args += [rope[0][:, :rw], rope[1][:, :rw]]
    widths = (qk_w, qk_w, MLA_HEADS * V_DIM, KV_RANK, ROPE_DIM)
    dtypes = (F32, F32, BF16, F32, F32)
    return pl.pallas_call(
        functools.partial(_mla_prep_kernel, rope=rope is not None),
        grid=(bsz, t_len // tt), in_specs=in_specs,
        out_specs=[pl.BlockSpec((1, tt, wd), lambda b, t: (b, t, 0)) for wd in widths],
        out_shape=[jax.ShapeDtypeStruct((bsz, t_len, wd), dt) for wd, dt in zip(widths, dtypes)],
        compiler_params=_cparams("parallel", "parallel"),
    )(*args)


def _mla_ctx(cache_ckv, cache_krope, layer, wkv, gk, consts):
    bsz, depth, tc, _ = cache_ckv.shape
    _, ek, tile = consts
    qk_w = MLA_HEADS * MLA_QK
    vw = MLA_HEADS * V_DIM
    args = [cache_ckv, cache_krope, wkv, gk, ek, tile]
    in_specs = [pl.BlockSpec((1, 1, tc, KV_RANK), lambda b: (b, layer, 0, 0)),
                pl.BlockSpec((1, 1, tc, ROPE_DIM), lambda b: (b, layer, 0, 0))]
    in_specs += [_const_spec(a) for a in args[2:]]
    return pl.pallas_call(
        _mla_ctx_kernel, grid=(bsz,), in_specs=in_specs,
        out_specs=[pl.BlockSpec((1, 1, tc, qk_w), lambda b: (b, 0, 0, 0)),
                   pl.BlockSpec((1, 1, tc, vw), lambda b: (b, 0, 0, 0))],
        out_shape=[jax.ShapeDtypeStruct((bsz, 1, tc, qk_w), F32), jax.ShapeDtypeStruct((bsz, 1, tc, vw), BF16)],
        compiler_params=_cparams("parallel"),
    )(*args)


def _ssd_prep_kernel(x_ref, cw_ref, cb_ref, dtb_ref, xs_ref, bm_ref, cm_ref, dt_ref):
    t_len = x_ref.shape[1]
    cw_total = cw_ref.shape[1]
    xbc = x_ref[0, :, BRANCH_W:BRANCH_W + cw_total]
    row = lax.broadcasted_iota(jnp.int32, xbc.shape, 0)
    prev1 = jnp.where(row >= 1, pltpu.roll(xbc, 1, 0), 0.0)
    next1 = jnp.where(row < t_len - 1, pltpu.roll(xbc, t_len - 1, 0), 0.0)
    next2 = jnp.where(row < t_len - 2, pltpu.roll(xbc, t_len - 2, 0), 0.0)
    cw = cw_ref[...]
    u = prev1 * cw[0:1] + xbc * cw[1:2] + next1 * cw[2:3] + next2 * cw[3:4] + cb_ref[...]
    u = u * jax.nn.sigmoid(u)
    xw = SSD_HEADS * SSD_HEAD_DIM
    bw = SSD_GROUPS * D_STATE
    xs_ref[0] = u[:, :xw]
    bm_ref[0] = u[:, xw:xw + bw]
    cm_ref[0] = u[:, xw + bw:]
    raw = x_ref[0, :, BRANCH_W + cw_total:] + dtb_ref[...]
    dt_ref[0] = jnp.maximum(raw, 0.0) + jnp.log1p(jnp.exp(-jnp.abs(raw)))


def _ssd_prep(g3, conv_w, conv_b, dt_bias):
    bsz, t_len, w = g3.shape
    cw = conv_w.shape[1]
    xw = SSD_HEADS * SSD_HEAD_DIM
    bw = SSD_GROUPS * D_STATE
    nh2 = 2 * SSD_HEADS
    widths = (xw, bw, bw, nh2)
    return pl.pallas_call(
        _ssd_prep_kernel, grid=(bsz,),
        in_specs=[pl.BlockSpec((1, t_len, w), lambda b: (b, 0, 0)),
                  pl.BlockSpec(conv_w.shape, lambda b: (0, 0)),
                  pl.BlockSpec((1, cw), lambda b: (0, 0)),
                  pl.BlockSpec((1, nh2), lambda b: (0, 0))],
        out_specs=[pl.BlockSpec((1, t_len, wd), lambda b: (b, 0, 0)) for wd in widths],
        out_shape=[jax.ShapeDtypeStruct((bsz, t_len, wd), F32) for wd in widths],
        compiler_params=_cparams("parallel"),
    )(g3, conv_w, conv_b.reshape(1, cw), dt_bias.reshape(1, nh2))


def _ssd_scan_kernel(*refs, has_s0, nc):
    if has_s0:
        xs_ref, bm_ref, cm_ref, dt_ref, dtt_ref, alogt_ref, s0_ref, y_ref, sfin_ref, st_ref = refs
    else:
        xs_ref, bm_ref, cm_ref, dt_ref, dtt_ref, alogt_ref, y_ref, sfin_ref, st_ref = refs
    d = pl.program_id(1)
    c = pl.program_id(2)
    chunk = xs_ref.shape[1]

    @pl.when(c == 0)
    def _():
        if has_s0:
            st_ref[...] = s0_ref[0, 0, 0]
        else:
            st_ref[...] = jnp.zeros_like(st_ref)

    a_col = -jnp.exp(alogt_ref[0])
    dt = dt_ref[0, 0]
    dta_t = dtt_ref[0, 0] * a_col
    ri = lax.broadcasted_iota(jnp.int32, (chunk, chunk), 0)
    ci = lax.broadcasted_iota(jnp.int32, (chunk, chunk), 1)
    sign = 1 - 2 * d
    mask = (ri - ci) * sign >= 0
    mask_t = jnp.where((ci - ri) * sign >= 0, 1.0, 0.0).astype(BF16)
    acs_rows = _sel_dot(dta_t, mask_t)

    xs = xs_ref[0]
    rep = SSD_HEADS // SSD_GROUPS
    cb = []
    for g in range(SSD_GROUPS):
        cg = cm_ref[0, :, g * D_STATE:(g + 1) * D_STATE].astype(BF16)
        bg = bm_ref[0, :, g * D_STATE:(g + 1) * D_STATE].astype(BF16)
        cb.append((cg, bg, _dot_nt(cg, bg)))

    for h in range(SSD_HEADS):
        cg, bg, cbg = cb[h // rep]
        dta_row = dta_t[h:h + 1, :]
        acs_col = jnp.sum(jnp.where(mask, dta_row, 0.0), axis=-1, keepdims=True)
        tot = jnp.sum(dta_row, axis=-1, keepdims=True)
        seg = acs_col - acs_rows[h:h + 1, :]
        decay = jnp.exp(jnp.where(mask, seg, -jnp.inf))
        xdt = xs[:, h * SSD_HEAD_DIM:(h + 1) * SSD_HEAD_DIM] * dt[:, h:h + 1]
        y = _dot((cbg * decay).astype(BF16), xdt.astype(BF16))
        s_h = st_ref[h]
        y = y + jnp.exp(acs_col) * _dot_nt(cg, s_h.astype(BF16))
        to_end = jnp.exp(tot - acs_col)
        st_ref[h] = s_h * jnp.exp(tot) + _dot_tn((xdt * to_end).astype(BF16), bg)
        y_ref[0, 0, :, h * SSD_HEAD_DIM:(h + 1) * SSD_HEAD_DIM] = y

    @pl.when(c == nc - 1)
    def _():
        sfin_ref[0, 0] = st_ref[...]


def _ssd_scan(xs, bm, cm, dt, a_log_l, s0=None, layer=0):
    bsz, t_len, xw = xs.shape
    chunk = CHUNK if t_len % CHUNK == 0 else t_len
    nc = t_len // chunk
    nh = SSD_HEADS
    bw = bm.shape[-1]
    dt4 = dt.reshape(bsz, t_len, 2, nh).transpose(0, 2, 1, 3)
    dtt = dt4.transpose(0, 1, 3, 2)

    def tok(b, d, c):
        return c + d * (nc - 1 - 2 * c)

    in_specs = [pl.BlockSpec((1, chunk, xw), lambda b, d, c: (b, tok(b, d, c), 0)),
                pl.BlockSpec((1, chunk, bw), lambda b, d, c: (b, tok(b, d, c), 0)),
                pl.BlockSpec((1, chunk, bw), lambda b, d, c: (b, tok(b, d, c), 0)),
                pl.BlockSpec((1, 1, chunk, nh), lambda b, d, c: (b, d, tok(b, d, c), 0)),
                pl.BlockSpec((1, 1, nh, chunk), lambda b, d, c: (b, d, 0, tok(b, d, c))),
                pl.BlockSpec((1, nh, 1), lambda b, d, c: (d, 0, 0))]
    args = [xs, bm, cm, dt4, dtt, a_log_l.reshape(2, nh, 1)]
    if s0 is not None:
        in_specs.append(pl.BlockSpec((1, 1, 1, nh, SSD_HEAD_DIM, D_STATE), lambda b, d, c: (b, layer, d, 0, 0, 0)))
        args.append(s0)
    return pl.pallas_call(
        functools.partial(_ssd_scan_kernel, has_s0=s0 is not None, nc=nc),
        grid=(bsz, 2, nc), in_specs=in_specs,
        out_specs=[pl.BlockSpec((1, 1, chunk, xw), lambda b, d, c: (d, b, tok(b, d, c), 0)),
                   pl.BlockSpec((1, 1, nh, SSD_HEAD_DIM, D_STATE), lambda b, d, c: (b, d, 0, 0, 0))],
        out_shape=[jax.ShapeDtypeStruct((2, bsz, t_len, xw), F32),
                   jax.ShapeDtypeStruct((bsz, 2, nh, SSD_HEAD_DIM, D_STATE), F32)],
        scratch_shapes=[pltpu.VMEM((nh, SSD_HEAD_DIM, D_STATE), F32)],
        compiler_params=_cparams("parallel", "arbitrary", "arbitrary"),
    )(*args)


def _ssd_post_kernel(yf_ref, yb_ref, xs_ref, z_ref, dsk_ref, g_ref, o_ref):
    z = z_ref[0]
    y = yf_ref[0, 0] + yb_ref[0, 0] + dsk_ref[...] * xs_ref[0]
    y = y * (z * jax.nn.sigmoid(z))
    y = y * lax.rsqrt(jnp.mean(y * y, axis=-1, keepdims=True) + EPS) * g_ref[...]
    o_ref[0] = y.astype(o_ref.dtype)


def _ssd_post(y2, xs, g3, d_skip, g_ssd):
    bsz, t_len, xw = xs.shape
    tt = _tile(t_len, 256)
    dsk = jnp.repeat(d_skip, SSD_HEAD_DIM).reshape(1, xw)
    return pl.pallas_call(
        _ssd_post_kernel, grid=(bsz, t_len // tt),
        in_specs=[pl.BlockSpec((1, 1, tt, xw), lambda b, t: (0, b, t, 0)),
                  pl.BlockSpec((1, 1, tt, xw), lambda b, t: (1, b, t, 0)),
                  pl.BlockSpec((1, tt, xw), lambda b, t: (b, t, 0)),
                  pl.BlockSpec((1, tt, xw), lambda b, t: (b, t, 0)),
                  pl.BlockSpec((1, xw), lambda b, t: (0, 0)),
                  pl.BlockSpec((1, xw), lambda b, t: (0, 0))],
        out_specs=pl.BlockSpec((1, tt, xw), lambda b, t: (b, t, 0)),
        out_shape=jax.ShapeDtypeStruct((bsz, t_len, xw), BF16),
        compiler_params=_cparams("parallel", "parallel"),
    )(y2, y2, xs, g3, dsk, g_ssd.reshape(1, xw))


def _merge_kernel(h_ref, wg_ref, ya_ref, yb_ref, yc_ref, yd_ref, wb_ref, o_ref):
    h = h_ref[0]
    acc = None
    for i, y_ref in enumerate((ya_ref, yb_ref, yc_ref, yd_ref)):
        gate = jax.nn.sigmoid(_dot(h, wg_ref[i]))
        term = gate * _dot(y_ref[0], wb_ref[i].astype(BF16))
        acc = term if acc is None else acc + term
    o_ref[0] = acc.astype(o_ref.dtype)


def _merge(h, wg, ys, w_branch, layer):
    bsz, t_len, d = h.shape
    tm = _tile(t_len, 512)
    tn = _tile(d, 512)
    y_spec = pl.BlockSpec((1, tm, BRANCH_W), lambda n, b, t: (b, t, 0))
    return pl.pallas_call(
        _merge_kernel, grid=(d // tn, bsz, t_len // tm),
        in_specs=[pl.BlockSpec((1, tm, d), lambda n, b, t: (b, t, 0)),
                  pl.BlockSpec((N_BRANCH, d, tn), lambda n, b, t: (0, 0, n)),
                  y_spec, y_spec, y_spec, y_spec,
                  pl.BlockSpec((None, N_BRANCH, BRANCH_W, tn), lambda n, b, t: (layer, 0, 0, n))],
        out_specs=pl.BlockSpec((1, tm, tn), lambda n, b, t: (b, t, n)),
        out_shape=jax.ShapeDtypeStruct((bsz, t_len, d), BF16),
        compiler_params=_cparams("parallel", "parallel", "parallel"),
    )(h, wg, *ys, w_branch)


def _moe_kernel(blk_e_ref, nused_ref, x_ref, wg_ref, wu_ref, bg_ref, bu_ref, wd_ref, bd_ref, o_ref):
    i = pl.program_id(0)
    f = pl.program_id(1)

    @pl.when(i < nused_ref[0])
    def _():
        x = x_ref[...]
        g = _dot(x, wg_ref[...].astype(BF16)) + bg_ref[...]
        u = _dot(x, wu_ref[...].astype(BF16)) + bu_ref[...]
        g = jnp.minimum(g, SWIGLU_LIMIT)
        u = jnp.clip(u, -SWIGLU_LIMIT, SWIGLU_LIMIT)
        act = g * jax.nn.sigmoid(SWIGLU_ALPHA * g) * (u + 1.0)
        part = _dot(act.astype(BF16), wd_ref[...].astype(BF16))

        @pl.when(f == 0)
        def _():
            o_ref[...] = part + bd_ref[...]

        @pl.when(f > 0)
        def _():
            o_ref[...] += part

    @pl.when((i >= nused_ref[0]) & (f == 0))
    def _():
        o_ref[...] = jnp.zeros_like(o_ref)


def _moe_blocks(xs, blk_e, nused, w_gu, b_gu, w_dn, b_dn, layer):
    cap, d = xs.shape
    n_exp, _, ff2 = w_gu.shape[1:]
    ff = ff2 // 2
    ts, fc = MOE_TOKENS, _tile(ff, MOE_FF_CHUNK)
    nf = ff // fc
    nblk = cap // ts

    def fcl(i, f, nu):
        return jnp.where(i < nu[0], f, nf - 1)

    grid_spec = pltpu.PrefetchScalarGridSpec(
        num_scalar_prefetch=2, grid=(nblk, nf),
        in_specs=[
            pl.BlockSpec((ts, d), lambda i, f, be, nu: (i, 0)),
            pl.BlockSpec((None, None, d, fc), lambda i, f, be, nu: (layer, be[i], 0, fcl(i, f, nu))),
            pl.BlockSpec((None, None, d, fc), lambda i, f, be, nu: (layer, be[i], 0, nf + fcl(i, f, nu))),
            pl.BlockSpec((None, None, 1, fc), lambda i, f, be, nu: (layer, be[i], 0, fcl(i, f, nu))),
            pl.BlockSpec((None, None, 1, fc), lambda i, f, be, nu: (layer, be[i], 0, nf + fcl(i, f, nu))),
            pl.BlockSpec((None, None, fc, d), lambda i, f, be, nu: (layer, be[i], fcl(i, f, nu), 0)),
            pl.BlockSpec((None, None, 1, d), lambda i, f, be, nu: (layer, be[i], 0, 0)),
        ],
        out_specs=pl.BlockSpec((ts, d), lambda i, f, be, nu: (i, 0)),
    )
    depth = w_gu.shape[0]
    return pl.pallas_call(
        _moe_kernel, grid_spec=grid_spec,
        out_shape=jax.ShapeDtypeStruct((cap, d), F32),
        compiler_params=_cparams("arbitrary", "arbitrary"),
    )(blk_e, nused, xs, w_gu, w_gu, b_gu.reshape(depth, n_exp, 1, ff2), b_gu.reshape(depth, n_exp, 1, ff2),
      w_dn, b_dn.reshape(depth, n_exp, 1, d))


def _moe(h2, logits, w_gu, b_gu, w_dn, b_dn, layer):
    n_tok, d = h2.shape
    n_exp = logits.shape[1]
    ts = MOE_TOKENS
    top_v, top_i = lax.top_k(logits, TOP_K)
    gate_w = jax.nn.softmax(top_v, axis=-1)
    n_asg = n_tok * TOP_K
    flat_e = top_i.reshape(-1)
    onehot = (flat_e[:, None] == jnp.arange(n_exp, dtype=flat_e.dtype)[None, :]).astype(jnp.int32)
    rank = jnp.take_along_axis(jnp.cumsum(onehot, axis=0), flat_e[:, None], axis=1)[:, 0] - 1
    counts = jnp.sum(onehot, axis=0)
    padded = (counts + ts - 1) // ts * ts
    pad_ends = jnp.cumsum(padded)
    pad_starts = pad_ends - padded
    pos = (pad_starts[flat_e] + rank).astype(jnp.int32)
    nblk = -(-n_asg // ts) + n_exp
    cap = nblk * ts
    flat_t = jnp.repeat(jnp.arange(n_tok, dtype=jnp.int32), TOP_K)
    slot_tok = jnp.zeros((cap,), jnp.int32).at[pos].set(flat_t)
    nused = (pad_ends[-1] // ts).astype(jnp.int32)
    blk_e = jnp.searchsorted(pad_ends, jnp.arange(nblk, dtype=jnp.int32) * ts, side='right')
    last_e = jnp.clip(blk_e[jnp.maximum(nused - 1, 0)], 0, n_exp - 1)
    blk_e = jnp.where(jnp.arange(nblk) < nused, jnp.clip(blk_e, 0, n_exp - 1), last_e).astype(jnp.int32)
    xs = jnp.take(h2, slot_tok, axis=0)
    yb = _moe_blocks(xs, blk_e, nused.reshape(1), w_gu, b_gu, w_dn, b_dn, layer)
    picked = jnp.take(yb, pos, axis=0).reshape(n_tok, TOP_K, d)
    return jnp.einsum('nk,nkd->nd', gate_w, picked)


def _token_mixers(h, lw, layer, lam_init, ctx):
    latent = ctx is not None
    bsz, t_len, _ = h.shape
    qkv = _mm(h, lw['w_qkv'])
    g2 = _mm(h, lw['w_lat'])
    g3 = _mm(h, lw['w_ssd'])
    rope = _rope_tables(t_len, BRANCH_W) if latent else None

    qa = _headnorm(qkv, 0, lw['g_na_q'], HEAD_DIM)
    ka = _headnorm(qkv, 1, lw['g_na_k'], HEAD_DIM)
    if latent:
        ya = _na_latent(qa, ka, qkv, 2, ctx['na_k'], ctx['na_v'], layer, lw['rpb'])
    else:
        ya = _attention(qa, ka, qkv, 2, heads=NA_HEADS, maps=1, pieces=_na_pieces, dv=HEAD_DIM,
                        scale=HEAD_DIM ** -0.5)

    qb = _headnorm(qkv, 3, lw['g_df_q'], HEAD_DIM, rope)
    kb = _headnorm(qkv, 4, lw['g_df_k'], HEAD_DIM, rope)
    diff = (lw['lam_q1'], lw['lam_k1'], lw['lam_q2'], lw['lam_k2'], lw['g_df_sub'], lam_init)
    df_ctx = (ctx['df_k'], ctx['df_v'], layer, 0) if latent else None
    yb = _attention(qb, kb, qkv, 5, heads=DF_HEADS, maps=2, pieces=_df_pieces, dv=2 * HEAD_DIM,
                    scale=HEAD_DIM ** -0.5, ctx=df_ctx, diff=diff)

    consts = _mla_consts()
    qc, kc, vc, ckv, krope = _mla_prep(g2, lw['g_mla_cq'], lw['g_mla_ckv'], lw['w_uq'], lw['w_ukv'],
                                       lw['g_mla_q'], lw['g_mla_k'], consts, rope)
    mla_ctx = None
    if latent:
        kcx, vcx = _mla_ctx(ctx['mla_ckv'], ctx['mla_krope'], layer, lw['w_ukv'], lw['g_mla_k'], consts)
        mla_ctx = (kcx, vcx, 0, 0)
    yc = _attention(qc, kc, vc, 0, heads=MLA_HEADS, maps=1, pieces=_mla_pieces, dv=V_DIM,
                    scale=MLA_QK ** -0.5, ctx=mla_ctx)

    xs, bm, cm, dt = _ssd_prep(g3, lw['conv_w'], lw['conv_b'], lw['dt_bias'])
    y2, s_fin = _ssd_scan(xs, bm, cm, dt, lw['a_log'], ctx['ssd'] if latent else None, layer)
    yd = _ssd_post(y2, xs, g3, lw['d_skip'], lw['g_ssd'])

    merged = _merge(h, lw['w_gate'], (ya, yb, yc, yd), lw['w_branch'], layer)
    if latent:
        return merged, None
    new = dict(
        na_k=ka.reshape(bsz, t_len, NA_HEADS, HEAD_DIM),
        na_v=qkv[..., 2 * BRANCH_W:3 * BRANCH_W].reshape(bsz, t_len, NA_HEADS, HEAD_DIM),
        df_k=kb.reshape(bsz, t_len, DF_HEADS, 2, HEAD_DIM),
        df_v=qkv[..., 5 * BRANCH_W:6 * BRANCH_W].reshape(bsz, t_len, DF_HEADS, 2 * HEAD_DIM),
        mla_ckv=ckv, mla_krope=krope, ssd=s_fin)
    return merged, new


def kernel(x_prompt, x_sample, cache_na_k, cache_na_v, cache_df_k, cache_df_v, cache_mla_ckv, cache_mla_krope, state_ssd, c, c_ctx, w_mod, b_mod, g_norm1, g_norm2, w_in, g_na_q, g_na_k, rpb, g_df_q, g_df_k, lam_q1, lam_k1, lam_q2, lam_k2, g_df_sub, g_mla_cq, g_mla_ckv, w_uq, w_ukv, g_mla_q, g_mla_k, conv_w, conv_b, dt_bias, a_log, d_skip, g_ssd, w_branch, w_o, w_router, b_router, w_gu, b_gu, w_dn, b_dn):
    xp, xs = x_prompt, x_sample
    depth, d, _ = w_in.shape
    dec_b, _, past = cache_na_k.shape[:3]
    n_p = xp.shape[0] * xp.shape[1]
    n_exp = w_router.shape[-1]

    n_rows = -(-(dec_b + 1) // 8) * 8
    cc = jnp.zeros((n_rows, d), F32).at[:dec_b].set(c).at[dec_b].set(c_ctx)
    mods = _mod_params(cc, w_mod, b_mod)

    ctx_all = dict(
        na_k=cache_na_k.reshape(dec_b, depth, past, NA_HEADS * HEAD_DIM),
        na_v=cache_na_v.reshape(dec_b, depth, past, NA_HEADS * HEAD_DIM),
        df_k=cache_df_k.reshape(dec_b, depth, past, DF_HEADS * 2 * HEAD_DIM),
        df_v=cache_df_v.reshape(dec_b, depth, past, DF_HEADS * 2 * HEAD_DIM),
        mla_ckv=cache_mla_ckv, mla_krope=cache_mla_krope, ssd=state_ssd)

    o_lat = 6 * BRANCH_W
    o_ssd = o_lat + Q_RANK + KV_RANK + ROPE_DIM
    o_gate = o_ssd + BRANCH_W + conv_w.shape[-1] + 2 * SSD_HEADS
    ne_pad = -(-n_exp // LANE) * LANE

    names = ('na_k', 'na_v', 'df_k', 'df_v', 'mla_ckv', 'mla_krope', 'ssd')
    new = {n: [] for n in names}
    for l in range(depth):
        lam_init = 0.8 - 0.6 * math.exp(-0.3 * l)
        wq, wkv, gq, gk = _mla_perm_weights(w_uq[l], w_ukv[l], g_mla_q[l], g_mla_k[l])
        w_l = w_in[l]
        lw = dict(
            w_qkv=w_l[:, :o_lat].astype(BF16), w_lat=w_l[:, o_lat:o_ssd].astype(BF16),
            w_ssd=w_l[:, o_ssd:o_gate].astype(BF16),
            w_gate=w_l[:, o_gate:].reshape(d, N_BRANCH, d).transpose(1, 0, 2).astype(BF16),
            g_na_q=g_na_q[l], g_na_k=g_na_k[l], rpb=rpb[l], g_df_q=g_df_q[l], g_df_k=g_df_k[l],
            lam_q1=lam_q1[l], lam_k1=lam_k1[l], lam_q2=lam_q2[l], lam_k2=lam_k2[l], g_df_sub=g_df_sub[l],
            g_mla_cq=g_mla_cq[l], g_mla_ckv=g_mla_ckv[l], w_uq=wq, w_ukv=wkv, g_mla_q=gq, g_mla_k=gk,
            conv_w=conv_w[l], conv_b=conv_b[l], dt_bias=dt_bias[l], a_log=a_log[l], d_skip=d_skip[l],
            g_ssd=g_ssd[l], w_branch=w_branch)
        mc = [m[:, None, :] for m in jnp.split(mods[l, :dec_b], 6, axis=-1)]
        mx = [m[:, None, :] for m in jnp.split(mods[l, dec_b:dec_b + 1], 6, axis=-1)]

        hp = _norm_mod(xp, g_norm1[l], mx[0], mx[1])
        mp, ctx_new = _token_mixers(hp, lw, l, lam_init, None)
        xp = _mm_residual(mp, w_o, l, xp, mx[2])
        hs = _norm_mod(xs, g_norm1[l], mc[0], mc[1])
        ms, _ = _token_mixers(hs, lw, l, lam_init, ctx_all)
        xs = _mm_residual(ms, w_o, l, xs, mc[2])

        wr = jnp.zeros((d, ne_pad), F32).at[:, :n_exp].set(w_router[l])
        br = jnp.zeros((1, ne_pad), F32).at[0, :n_exp].set(b_router[l])
        hp2, lg_p = _norm_mod(xp, g_norm2[l], mx[3], mx[4], router=(wr, br))
        hs2, lg_s = _norm_mod(xs, g_norm2[l], mc[3], mc[4], router=(wr, br))
        h2 = jnp.concatenate([hp2.reshape(-1, d), hs2.reshape(-1, d)], axis=0)
        logits = jnp.concatenate([lg_p.reshape(-1, ne_pad), lg_s.reshape(-1, ne_pad)], axis=0)[:, :n_exp]
        m = _moe(h2, logits, w_gu, b_gu, w_dn, b_dn, l)
        xp = xp + mx[5] * m[:n_p].reshape(xp.shape)
        xs = xs + mc[5] * m[n_p:].reshape(xs.shape)
        for n in names:
            new[n].append(ctx_new[n])
    return (xp, xs) + tuple(jnp.stack(new[n], axis=1) for n in names)
```

```python
import functools
import math

import jax
import jax.numpy as jnp
from jax import lax
import numpy as np
from jax.experimental import pallas as pl
from jax.experimental.pallas import tpu as pltpu

F32 = jnp.float32
BF16 = jnp.bfloat16

GRID_W = 64
HEAD_DIM = 64
ROPE_BASE = 10000.0
EPS = 1e-6
NA_HEADS = 8
WIN_R = 8
WIN_C = 16
DF_HEADS = 4
MLA_HEADS = 4
Q_RANK = 384
KV_RANK = 128
NOPE_DIM = 128
ROPE_DIM = 64
V_DIM = 128
MLA_QK = NOPE_DIM + ROPE_DIM
SSD_HEADS = 8
SSD_HEAD_DIM = 64
SSD_GROUPS = 2
D_STATE = 128
CHUNK = 128
TOP_K = 4
SWIGLU_ALPHA = 1.702
SWIGLU_LIMIT = 7.0
N_BRANCH = 4
BRANCH_W = 512

V7X_VMEM_BYTES = 64 * 1024 * 1024
VMEM_LIMIT = V7X_VMEM_BYTES - 8 * 1024 * 1024
LANE = 128

NA_ROWS_PER_STEP = 4
MOE_TOKENS = 512
MOE_FF_CHUNK = 512


def _cparams(*sem):
    return pltpu.CompilerParams(dimension_semantics=sem, vmem_limit_bytes=VMEM_LIMIT)


def _tile(n, pref):
    t = min(n, pref)
    while n % t:
        t //= 2
    return t


def _dot(a, b):
    return jnp.dot(a, b, preferred_element_type=F32)


def _dot_nt(a, b):
    return lax.dot_general(a, b, (((1,), (1,)), ((), ())), preferred_element_type=F32)


def _dot_tn(a, b):
    return lax.dot_general(a, b, (((0,), (0,)), ((), ())), preferred_element_type=F32)


def _split3(x):
    h = x.astype(BF16)
    r = x - h.astype(F32)
    m = r.astype(BF16)
    lo = (r - m.astype(F32)).astype(BF16)
    return h, m, lo


def _sel_dot(x, sel):
    h, m, lo = _split3(x)
    return _dot(h, sel) + _dot(m, sel) + _dot(lo, sel)


def _dot_hi(a, b):
    a1, a2, a3 = _split3(a)
    b1, b2, b3 = _split3(b)
    return (_dot(a1, b1) + (_dot(a1, b2) + _dot(a2, b1))
            + (_dot(a1, b3) + _dot(a2, b2) + _dot(a3, b1)))


def _rope_rotate(y, cos, sin):
    w = y.shape[-1]
    lane = lax.broadcasted_iota(jnp.int32, y.shape, 1)
    partner = jnp.where((lane % 32) < 16, pltpu.roll(y, w - 16, 1), pltpu.roll(y, 16, 1))
    return y * cos + partner * sin


def _rope_tables(t_len, width):
    half = HEAD_DIM // 2
    inv = jnp.asarray(ROPE_BASE ** (-np.arange(0, half, 2) / half), F32)
    t = np.arange(t_len)
    cols = []
    sins = []
    for pos in (t // GRID_W, t % GRID_W):
        ang = jnp.asarray(pos, F32)[:, None] * inv[None, :]
        c, s = jnp.cos(ang), jnp.sin(ang)
        cols += [c, c]
        sins += [-s, s]
    cos = jnp.concatenate(cols, axis=-1)
    sin = jnp.concatenate(sins, axis=-1)
    rep = width // HEAD_DIM
    return jnp.tile(cos, (1, rep)), jnp.tile(sin, (1, rep))


def _group_selector(width, group):
    g = np.arange(width) // group
    return jnp.asarray(g[:, None] == g[None, :], BF16)


def _mod_kernel(c_ref, w_ref, b_ref, o_ref):
    c = c_ref[...]
    s = (c * jax.nn.sigmoid(c)).astype(BF16)
    o_ref[0] = _dot(s, w_ref[0].astype(BF16)) + b_ref[0]


def _mod_params(cc, w_mod, b_mod):
    depth, d, n = w_mod.shape
    rows = cc.shape[0]
    tn = _tile(n, 1024)
    return pl.pallas_call(
        _mod_kernel,
        grid=(depth, n // tn),
        in_specs=[pl.BlockSpec((rows, d), lambda l, j: (0, 0)),
                  pl.BlockSpec((1, d, tn), lambda l, j: (l, 0, j)),
                  pl.BlockSpec((1, 1, tn), lambda l, j: (l, 0, j))],
        out_specs=pl.BlockSpec((1, rows, tn), lambda l, j: (l, 0, j)),
        out_shape=jax.ShapeDtypeStruct((depth, rows, n), F32),
        compiler_params=_cparams("parallel", "parallel"),
    )(cc, w_mod, b_mod.reshape(depth, 1, n))


def _norm_mod_body(x_ref, g_ref, sh_ref, sc_ref):
    x = x_ref[0]
    y = x * lax.rsqrt(jnp.mean(x * x, axis=-1, keepdims=True) + EPS)
    y = y * g_ref[...]
    return y * (1.0 + sc_ref[0]) + sh_ref[0]


def _norm_mod_kernel(x_ref, g_ref, sh_ref, sc_ref, o_ref):
    o_ref[0] = _norm_mod_body(x_ref, g_ref, sh_ref, sc_ref).astype(o_ref.dtype)


def _norm_mod_router_kernel(x_ref, g_ref, sh_ref, sc_ref, wr_ref, br_ref, o_ref, lg_ref):
    y = _norm_mod_body(x_ref, g_ref, sh_ref, sc_ref)
    o_ref[0] = y.astype(o_ref.dtype)
    lg_ref[0] = _dot_hi(y, wr_ref[...]) + br_ref[...]


def _mod_spec(m, d):
    if m.shape[0] == 1:
        return pl.BlockSpec((1, 1, d), lambda b, t: (0, 0, 0))
    return pl.BlockSpec((1, 1, d), lambda b, t: (b, 0, 0))


def _norm_mod(x, g, shift, scale, router=None):
    bsz, t_len, d = x.shape
    tt = _tile(t_len, 256)
    in_specs = [pl.BlockSpec((1, tt, d), lambda b, t: (b, t, 0)),
                pl.BlockSpec((1, d), lambda b, t: (0, 0)),
                _mod_spec(shift, d), _mod_spec(scale, d)]
    out_spec = pl.BlockSpec((1, tt, d), lambda b, t: (b, t, 0))
    out_shape = jax.ShapeDtypeStruct((bsz, t_len, d), BF16 if router is None else F32)
    args = [x, g.reshape(1, d), shift, scale]
    if router is None:
        return pl.pallas_call(
            _norm_mod_kernel, grid=(bsz, t_len // tt), in_specs=in_specs, out_specs=out_spec,
            out_shape=out_shape, compiler_params=_cparams("parallel", "parallel"))(*args)
    wr, br = router
    ne = wr.shape[1]
    in_specs += [pl.BlockSpec((d, ne), lambda b, t: (0, 0)), pl.BlockSpec((1, ne), lambda b, t: (0, 0))]
    return pl.pallas_call(
        _norm_mod_router_kernel, grid=(bsz, t_len // tt), in_specs=in_specs,
        out_specs=[out_spec, pl.BlockSpec((1, tt, ne), lambda b, t: (b, t, 0))],
        out_shape=[out_shape, jax.ShapeDtypeStruct((bsz, t_len, ne), F32)],
        compiler_params=_cparams("parallel", "parallel"))(*args, wr, br)


def _mm_kernel(x_ref, w_ref, o_ref):
    o_ref[0] = _dot(x_ref[0].astype(BF16), w_ref[...].astype(BF16)).astype(o_ref.dtype)


def _mm_res_kernel(x_ref, w_ref, r_ref, g_ref, o_ref):
    o_ref[0] = r_ref[0] + g_ref[0] * _dot(x_ref[0].astype(BF16), w_ref[...].astype(BF16))


def _w_spec(w, layer, k, tn):
    if w.ndim == 3:
        return pl.BlockSpec((None, k, tn), lambda n, b, t: (layer, 0, n))
    return pl.BlockSpec((k, tn), lambda n, b, t: (0, n))


def _mm(x, w, layer=0, out_dtype=F32, tm_pref=512, tn_pref=512):
    bsz, t_len, k = x.shape
    n = w.shape[-1]
    tm = _tile(t_len, tm_pref)
    tn = n if n % LANE else _tile(n, tn_pref)
    return pl.pallas_call(
        _mm_kernel,
        grid=(n // tn, bsz, t_len // tm),
        in_specs=[pl.BlockSpec((1, tm, k), lambda n_, b, t: (b, t, 0)), _w_spec(w, layer, k, tn)],
        out_specs=pl.BlockSpec((1, tm, tn), lambda n_, b, t: (b, t, n_)),
        out_shape=jax.ShapeDtypeStruct((bsz, t_len, n), out_dtype),
        compiler_params=_cparams("parallel", "parallel", "parallel"),
    )(x, w)


def _mm_residual(x, w, layer, res, gate):
    bsz, t_len, k = x.shape
    n = w.shape[-1]
    tm = _tile(t_len, 512)
    tn = _tile(n, 512)
    if gate.shape[0] == 1:
        g_spec = pl.BlockSpec((1, 1, tn), lambda n_, b, t: (0, 0, n_))
    else:
        g_spec = pl.BlockSpec((1, 1, tn), lambda n_, b, t: (b, 0, n_))
    return pl.pallas_call(
        _mm_res_kernel,
        grid=(n // tn, bsz, t_len // tm),
        in_specs=[pl.BlockSpec((1, tm, k), lambda n_, b, t: (b, t, 0)), _w_spec(w, layer, k, tn),
                  pl.BlockSpec((1, tm, tn), lambda n_, b, t: (b, t, n_)), g_spec],
        out_specs=pl.BlockSpec((1, tm, tn), lambda n_, b, t: (b, t, n_)),
        out_shape=jax.ShapeDtypeStruct((bsz, t_len, n), F32),
        compiler_params=_cparams("parallel", "parallel", "parallel"),
    )(x, w, res, gate)


def _headnorm_kernel(*refs, group, rope):
    if rope:
        x_ref, g_ref, e_ref, cos_ref, sin_ref, o_ref = refs
    else:
        x_ref, g_ref, e_ref, o_ref = refs
    x = x_ref[0]
    ms = _sel_dot(x * x, e_ref[...]) * (1.0 / group)
    y = x * lax.rsqrt(ms + EPS) * g_ref[...]
    if rope:
        y = _rope_rotate(y, cos_ref[...], sin_ref[...])
    o_ref[0] = y


def _headnorm(x, col, gain, group, rope=None):
    bsz, t_len, _ = x.shape
    w = BRANCH_W
    tt = _tile(t_len, 256)
    g = jnp.tile(gain, w // gain.shape[0]).reshape(1, w)
    in_specs = [pl.BlockSpec((1, tt, w), lambda b, t: (b, t, col)),
                pl.BlockSpec((1, w), lambda b, t: (0, 0)),
                pl.BlockSpec((w, w), lambda b, t: (0, 0))]
    args = [x, g, _group_selector(w, group)]
    if rope is not None:
        in_specs += [pl.BlockSpec((tt, w), lambda b, t: (t, 0))] * 2
        args += list(rope)
    return pl.pallas_call(
        functools.partial(_headnorm_kernel, group=group, rope=rope is not None),
        grid=(bsz, t_len // tt), in_specs=in_specs,
        out_specs=pl.BlockSpec((1, tt, w), lambda b, t: (b, t, 0)),
        out_shape=jax.ShapeDtypeStruct((bsz, t_len, w), F32),
        compiler_params=_cparams("parallel", "parallel"),
    )(*args)


def _attn_kernel(*refs, heads, maps, pieces, dv, scale, has_ctx, diff, post_scale):
    it = iter(refs)
    q_ref, k_ref, v_ref = next(it), next(it), next(it)
    kc_ref = vc_ref = None
    if has_ctx:
        kc_ref, vc_ref = next(it), next(it)
    if diff is not None:
        lq1, lk1, lq2, lk2, gsub_ref = next(it), next(it), next(it), next(it), next(it)
    o_ref = next(it)

    q = q_ref[0].astype(BF16)
    k = k_ref[0].astype(BF16)
    v = v_ref[0].astype(BF16)
    if has_ctx:
        kc = kc_ref[0, 0].astype(BF16)
        vc = vc_ref[0, 0].astype(BF16)
    if diff is not None:
        lam = (jnp.exp(jnp.sum(lq1[...] * lk1[...], axis=-1, keepdims=True))
               - jnp.exp(jnp.sum(lq2[...] * lk2[...], axis=-1, keepdims=True)) + diff)

    for h in range(heads):
        w_own = w_ctx = None
        for m in range(maps):
            s = s_c = None
            for (qo, ko, d) in pieces(h, m):
                part = _dot_nt(q[:, qo:qo + d], k[:, ko:ko + d])
                s = part if s is None else s + part
                if has_ctx:
                    part_c = _dot_nt(q[:, qo:qo + d], kc[:, ko:ko + d])
                    s_c = part_c if s_c is None else s_c + part_c
            s = s * scale
            mx = jnp.max(s, axis=-1, keepdims=True)
            if has_ctx:
                s_c = s_c * scale
                mx = jnp.maximum(mx, jnp.max(s_c, axis=-1, keepdims=True))
            p = jnp.exp(s - mx)
            den = jnp.sum(p, axis=-1, keepdims=True)
            if has_ctx:
                p_c = jnp.exp(s_c - mx)
                den = den + jnp.sum(p_c, axis=-1, keepdims=True)
            inv = 1.0 / den
            if m == 1:
                inv = -lam * inv
            w_own = p * inv if w_own is None else w_own + p * inv
            if has_ctx:
                w_ctx = p_c * inv if w_ctx is None else w_ctx + p_c * inv
        o = _dot(w_own.astype(BF16), v[:, h * dv:(h + 1) * dv])
        if has_ctx:
            o = o + _dot(w_ctx.astype(BF16), vc[:, h * dv:(h + 1) * dv])
        if diff is not None:
            o = o * lax.rsqrt(jnp.mean(o * o, axis=-1, keepdims=True) + EPS) * gsub_ref[...] * post_scale
        o_ref[0, :, h * dv:(h + 1) * dv] = o.astype(o_ref.dtype)


def _attention(q, k, v, v_col, *, heads, maps, pieces, dv, scale, ctx=None, diff=None):
    bsz, t_len, wq = q.shape
    wk = k.shape[-1]
    wv = heads * dv
    tq = _tile(t_len, 256)
    in_specs = [pl.BlockSpec((1, tq, wq), lambda b, t: (b, t, 0)),
                pl.BlockSpec((1, t_len, wk), lambda b, t: (b, 0, 0)),
                pl.BlockSpec((1, t_len, wv), lambda b, t: (b, 0, v_col))]
    args = [q, k, v]
    if ctx is not None:
        kc, vc, layer, vc_col = ctx
        tc = kc.shape[2]
        in_specs += [pl.BlockSpec((1, 1, tc, wk), lambda b, t: (b, layer, 0, 0)),
                     pl.BlockSpec((1, 1, tc, wv), lambda b, t: (b, layer, 0, vc_col))]
        args += [kc, vc]
    lam_init = 0.0
    if diff is not None:
        lq1, lk1, lq2, lk2, gsub, lam_init = diff
        for a in (lq1, lk1, lq2, lk2, gsub):
            in_specs.append(pl.BlockSpec((1, a.shape[0]), lambda b, t: (0, 0)))
            args.append(a.reshape(1, -1))
    kern = functools.partial(
        _attn_kernel, heads=heads, maps=maps, pieces=pieces, dv=dv, scale=scale,
        has_ctx=ctx is not None, diff=lam_init if diff is not None else None,
        post_scale=1.0 - lam_init)
    return pl.pallas_call(
        kern, grid=(bsz, t_len // tq), in_specs=in_specs,
        out_specs=pl.BlockSpec((1, tq, wv), lambda b, t: (b, t, 0)),
        out_shape=jax.ShapeDtypeStruct((bsz, t_len, wv), BF16),
        compiler_params=_cparams("parallel", "parallel"),
    )(*args)


def _na_pieces(h, m):
    return [(h * HEAD_DIM, h * HEAD_DIM, HEAD_DIM)]


def _df_pieces(h, m):
    o = (2 * h + m) * HEAD_DIM
    return [(o, o, HEAD_DIM)]


def _mla_pieces(h, m):
    nope_w = MLA_HEADS * NOPE_DIM
    return [(h * NOPE_DIM, h * NOPE_DIM, NOPE_DIM),
            (nope_w + h * ROPE_DIM, nope_w + h * ROPE_DIM, ROPE_DIM)]


def _na_latent_kernel(q_ref, k_ref, v_ref, kc_ref, vc_ref, bias_ref, o_ref, *, rows, kr, rps, scale):
    kc = kc_ref[0, 0].astype(BF16)
    vc = vc_ref[0, 0].astype(BF16)
    for rr in range(rps):
        r = pl.program_id(1) * rps + rr
        start_row = jnp.clip(r - WIN_R // 2, 0, rows - kr)
        start = pl.multiple_of(start_row * GRID_W, GRID_W)
        q = q_ref[0, rr * GRID_W:(rr + 1) * GRID_W, :].astype(BF16)
        kl = k_ref[0, pl.ds(start, kr * GRID_W), :].astype(BF16)
        vl = v_ref[0, pl.ds(start, kr * GRID_W), :].astype(BF16)
        ro0 = start_row - r + WIN_R - 1
        s_heads = []
        for h in range(NA_HEADS):
            sl = slice(h * HEAD_DIM, (h + 1) * HEAD_DIM)
            bias = jnp.concatenate([bias_ref[h, ro0 + 2 * j] for j in range(kr // 2)], axis=-1)
            s_loc = _dot_nt(q[:, sl], kl[:, sl]) * scale + bias
            s_ctx = _dot_nt(q[:, sl], kc[:, sl]) * scale
            s_heads.append(jnp.concatenate([s_loc, s_ctx], axis=-1))
        s = jnp.concatenate(s_heads, axis=0)
        p = jnp.exp(s - jnp.max(s, axis=-1, keepdims=True))
        pn = (p * (1.0 / jnp.sum(p, axis=-1, keepdims=True))).astype(BF16)
        n_loc = kr * GRID_W
        for h in range(NA_HEADS):
            sl = slice(h * HEAD_DIM, (h + 1) * HEAD_DIM)
            ph = pn[h * GRID_W:(h + 1) * GRID_W]
            o = _dot(ph[:, :n_loc], vl[:, sl]) + _dot(ph[:, n_loc:], vc[:, sl])
            o_ref[0, rr * GRID_W:(rr + 1) * GRID_W, sl] = o.astype(o_ref.dtype)


def _na_bias_table(rpb_l):
    qc = np.arange(GRID_W)
    kc = np.arange(GRID_W)
    qstart = np.clip(qc - WIN_C // 2, 0, GRID_W - WIN_C)
    valid = (kc[None, :] >= qstart[:, None]) & (kc[None, :] < qstart[:, None] + WIN_C)
    c_off = np.clip(kc[None, :] - qc[:, None] + WIN_C - 1, 0, 2 * WIN_C - 2)
    onehot = (c_off[None] == np.arange(2 * WIN_C - 1)[:, None, None]) & valid[None]
    tab = jnp.einsum('hrc,cqk->hrqk', rpb_l.astype(F32), jnp.asarray(onehot, F32),
                     precision=lax.Precision.HIGHEST)
    tab = jnp.where(valid[None, None], tab, -jnp.inf)
    return jnp.concatenate([tab[:, :-1], tab[:, 1:]], axis=-1)


def _na_latent(q, k, v, v_col, kc, vc, layer, rpb_l):
    bsz, t_len, w = q.shape
    rows = t_len // GRID_W
    kr = min(WIN_R, rows)
    assert kr % 2 == 0, "the bias table pairs key rows"
    tc = kc.shape[2]
    bias = _na_bias_table(rpb_l)
    rps = _tile(rows, NA_ROWS_PER_STEP)
    return pl.pallas_call(
        functools.partial(_na_latent_kernel, rows=rows, kr=kr, rps=rps, scale=HEAD_DIM ** -0.5),
        grid=(bsz, rows // rps),
        in_specs=[pl.BlockSpec((1, rps * GRID_W, w), lambda b, r: (b, r, 0)),
                  pl.BlockSpec((1, t_len, w), lambda b, r: (b, 0, 0)),
                  pl.BlockSpec((1, t_len, w), lambda b, r: (b, 0, v_col)),
                  pl.BlockSpec((1, 1, tc, w), lambda b, r: (b, layer, 0, 0)),
                  pl.BlockSpec((1, 1, tc, w), lambda b, r: (b, layer, 0, 0)),
                  _const_spec(bias)],
        out_specs=pl.BlockSpec((1, rps * GRID_W, w), lambda b, r: (b, r, 0)),
        out_shape=jax.ShapeDtypeStruct((bsz, t_len, w), BF16),
        compiler_params=_cparams("parallel", "parallel"),
    )(q, k, v, kc, vc, bias)


def _mla_k_part(ckv_bf16, krope, wukv_ref, gk_ref, ek_ref, tile_ref, rope_tabs):
    nope_w = MLA_HEADS * NOPE_DIM
    kv = _dot(ckv_bf16, wukv_ref[...])
    k_nope = kv[:, :nope_w]
    kr_t = _sel_dot(krope, tile_ref[...])
    sq = jnp.concatenate([k_nope * k_nope, krope * krope], axis=-1)
    ms = _sel_dot(sq, ek_ref[...]) * (1.0 / MLA_QK)
    rs = lax.rsqrt(ms + EPS)
    g = gk_ref[...]
    kn = k_nope * rs[:, :nope_w] * g[:, :nope_w]
    kro = kr_t * rs[:, nope_w:] * g[:, nope_w:]
    if rope_tabs is not None:
        kro = _rope_rotate(kro, rope_tabs[0][...], rope_tabs[1][...])
    return jnp.concatenate([kn, kro], axis=-1), kv[:, nope_w:]


def _mla_prep_kernel(*refs, rope):
    it = iter(refs)
    x_ref = next(it)
    gcq_ref, gckv_ref, wuq_ref, wukv_ref, gq_ref, gk_ref, eq_ref, ek_ref, tile_ref = (next(it) for _ in range(9))
    tabs = (next(it), next(it)) if rope else None
    q_ref, k_ref, v_ref, ckv_ref, kr_ref = (next(it) for _ in range(5))
    nope_w = MLA_HEADS * NOPE_DIM

    x = x_ref[0]
    dq = x[:, :Q_RANK]
    cq = dq * lax.rsqrt(jnp.mean(dq * dq, axis=-1, keepdims=True) + EPS) * gcq_ref[...]
    dkv = x[:, Q_RANK:Q_RANK + KV_RANK]
    ckv = dkv * lax.rsqrt(jnp.mean(dkv * dkv, axis=-1, keepdims=True) + EPS) * gckv_ref[...]
    krope = x[:, Q_RANK + KV_RANK:]
    ckv_ref[0] = ckv
    kr_ref[0] = krope

    qr = _dot(cq.astype(BF16), wuq_ref[...])
    ms = _sel_dot(qr * qr, eq_ref[...]) * (1.0 / MLA_QK)
    qn = qr * lax.rsqrt(ms + EPS) * gq_ref[...]
    if rope:
        q_ref[0] = jnp.concatenate(
            [qn[:, :nope_w], _rope_rotate(qn[:, nope_w:], tabs[0][...], tabs[1][...])], axis=-1)
    else:
        q_ref[0] = qn
    kn, v = _mla_k_part(ckv.astype(BF16), krope, wukv_ref, gk_ref, ek_ref, tile_ref, tabs)
    k_ref[0] = kn
    v_ref[0] = v.astype(v_ref.dtype)


def _mla_ctx_kernel(ckv_ref, kr_ref, wukv_ref, gk_ref, ek_ref, tile_ref, k_ref, v_ref):
    kn, v = _mla_k_part(ckv_ref[0, 0].astype(BF16), kr_ref[0, 0], wukv_ref, gk_ref, ek_ref, tile_ref, None)
    k_ref[0, 0] = kn
    v_ref[0, 0] = v.astype(v_ref.dtype)


def _mla_consts():
    nope_w = MLA_HEADS * NOPE_DIM
    qk_w = nope_w + MLA_HEADS * ROPE_DIM
    head_q = np.concatenate([np.arange(nope_w) // NOPE_DIM, np.arange(MLA_HEADS * ROPE_DIM) // ROPE_DIM])
    eq = head_q[:, None] == head_q[None, :]
    ek = np.concatenate([eq[:nope_w], np.ones((ROPE_DIM, qk_w), bool)], axis=0)
    tile = np.arange(ROPE_DIM)[:, None] == (np.arange(MLA_HEADS * ROPE_DIM) % ROPE_DIM)[None, :]
    return jnp.asarray(eq, BF16), jnp.asarray(ek, BF16), jnp.asarray(tile, BF16)


def _mla_perm_weights(w_uq_l, w_ukv_l, g_q, g_k):
    wq = w_uq_l.reshape(Q_RANK, MLA_HEADS, MLA_QK)
    wq = jnp.concatenate([wq[:, :, :NOPE_DIM].reshape(Q_RANK, -1), wq[:, :, NOPE_DIM:].reshape(Q_RANK, -1)], axis=1)
    wkv = w_ukv_l.reshape(KV_RANK, MLA_HEADS, NOPE_DIM + V_DIM)
    wkv = jnp.concatenate([wkv[:, :, :NOPE_DIM].reshape(KV_RANK, -1), wkv[:, :, NOPE_DIM:].reshape(KV_RANK, -1)], axis=1)

    def gain(g):
        return jnp.concatenate([jnp.tile(g[:NOPE_DIM], MLA_HEADS), jnp.tile(g[NOPE_DIM:], MLA_HEADS)]).reshape(1, -1)

    return wq.astype(BF16), wkv.astype(BF16), gain(g_q), gain(g_k)


def _const_spec(a):
    nd = a.ndim
    return pl.BlockSpec(a.shape, lambda *_: (0,) * nd)


def _mla_prep(x, g_cq, g_ckv, wq, wkv, gq, gk, consts, rope):
    bsz, t_len, w = x.shape
    tt = _tile(t_len, 256)
    eq, ek, tile = consts
    qk_w = MLA_HEADS * MLA_QK
    args = [x, g_cq.reshape(1, -1), g_ckv.reshape(1, -1), wq, wkv, gq, gk, eq, ek, tile]
    in_specs = [pl.BlockSpec((1, tt, w), lambda b, t: (b, t, 0))] + [_const_spec(a) for a in args[1:]]
    if rope is not None:
        rw = MLA_HEADS * ROPE_DIM
        in_specs += [pl.BlockSpec((tt, rw), lambda b, t: (t, 0))] * 2
        args += [rope[0][:, :rw], rope[1][:, :rw]]
    widths = (qk_w, qk_w, MLA_HEADS * V_DIM, KV_RANK, ROPE_DIM)
    dtypes = (F32, F32, BF16, F32, F32)
    return pl.pallas_call(
        functools.partial(_mla_prep_kernel, rope=rope is not None),
        grid=(bsz, t_len // tt), in_specs=in_specs,
        out_specs=[pl.BlockSpec((1, tt, wd), lambda b, t: (b, t, 0)) for wd in widths],
        out_shape=[jax.ShapeDtypeStruct((bsz, t_len, wd), dt) for wd, dt in zip(widths, dtypes)],
        compiler_params=_cparams("parallel", "parallel"),
    )(*args)


def _mla_ctx(cache_ckv, cache_krope, layer, wkv, gk, consts):
    bsz, depth, tc, _ = cache_ckv.shape
    _, ek, tile = consts
    qk_w = MLA_HEADS * MLA_QK
    vw = MLA_HEADS * V_DIM
    args = [cache_ckv, cache_krope, wkv, gk, ek, tile]
    in_specs = [pl.BlockSpec((1, 1, tc, KV_RANK), lambda b: (b, layer, 0, 0)),
                pl.BlockSpec((1, 1, tc, ROPE_DIM), lambda b: (b, layer, 0, 0))]
    in_specs += [_const_spec(a) for a in args[2:]]
    return pl.pallas_call(
        _mla_ctx_kernel, grid=(bsz,), in_specs=in_specs,
        out_specs=[pl.BlockSpec((1, 1, tc, qk_w), lambda b: (b, 0, 0, 0)),
                   pl.BlockSpec((1, 1, tc, vw), lambda b: (b, 0, 0, 0))],
        out_shape=[jax.ShapeDtypeStruct((bsz, 1, tc, qk_w), F32), jax.ShapeDtypeStruct((bsz, 1, tc, vw), BF16)],
        compiler_params=_cparams("parallel"),
    )(*args)


def _ssd_prep_kernel(x_ref, cw_ref, cb_ref, dtb_ref, xs_ref, bm_ref, cm_ref, dt_ref):
    t_len = x_ref.shape[1]
    cw_total = cw_ref.shape[1]
    xbc = x_ref[0, :, BRANCH_W:BRANCH_W + cw_total]
    row = lax.broadcasted_iota(jnp.int32, xbc.shape, 0)
    prev1 = jnp.where(row >= 1, pltpu.roll(xbc, 1, 0), 0.0)
    next1 = jnp.where(row < t_len - 1, pltpu.roll(xbc, t_len - 1, 0), 0.0)
    next2 = jnp.where(row < t_len - 2, pltpu.roll(xbc, t_len - 2, 0), 0.0)
    cw = cw_ref[...]
    u = prev1 * cw[0:1] + xbc * cw[1:2] + next1 * cw[2:3] + next2 * cw[3:4] + cb_ref[...]
    u = u * jax.nn.sigmoid(u)
    xw = SSD_HEADS * SSD_HEAD_DIM
    bw = SSD_GROUPS * D_STATE
    xs_ref[0] = u[:, :xw]
    bm_ref[0] = u[:, xw:xw + bw]
    cm_ref[0] = u[:, xw + bw:]
    raw = x_ref[0, :, BRANCH_W + cw_total:] + dtb_ref[...]
    dt_ref[0] = jnp.maximum(raw, 0.0) + jnp.log1p(jnp.exp(-jnp.abs(raw)))


def _ssd_prep(g3, conv_w, conv_b, dt_bias):
    bsz, t_len, w = g3.shape
    cw = conv_w.shape[1]
    xw = SSD_HEADS * SSD_HEAD_DIM
    bw = SSD_GROUPS * D_STATE
    nh2 = 2 * SSD_HEADS
    widths = (xw, bw, bw, nh2)
    return pl.pallas_call(
        _ssd_prep_kernel, grid=(bsz,),
        in_specs=[pl.BlockSpec((1, t_len, w), lambda b: (b, 0, 0)),
                  pl.BlockSpec(conv_w.shape, lambda b: (0, 0)),
                  pl.BlockSpec((1, cw), lambda b: (0, 0)),
                  pl.BlockSpec((1, nh2), lambda b: (0, 0))],
        out_specs=[pl.BlockSpec((1, t_len, wd), lambda b: (b, 0, 0)) for wd in widths],
        out_shape=[jax.ShapeDtypeStruct((bsz, t_len, wd), F32) for wd in widths],
        compiler_params=_cparams("parallel"),
    )(g3, conv_w, conv_b.reshape(1, cw), dt_bias.reshape(1, nh2))


def _ssd_scan_kernel(*refs, has_s0, nc):
    if has_s0:
        xs_ref, bm_ref, cm_ref, dt_ref, dtt_ref, alogt_ref, s0_ref, y_ref, sfin_ref, st_ref = refs
    else:
        xs_ref, bm_ref, cm_ref, dt_ref, dtt_ref, alogt_ref, y_ref, sfin_ref, st_ref = refs
    d = pl.program_id(1)
    c = pl.program_id(2)
    chunk = xs_ref.shape[1]

    @pl.when(c == 0)
    def _():
        if has_s0:
            st_ref[...] = s0_ref[0, 0, 0]
        else:
            st_ref[...] = jnp.zeros_like(st_ref)

    a_col = -jnp.exp(alogt_ref[0])
    dt = dt_ref[0, 0]
    dta_t = dtt_ref[0, 0] * a_col
    ri = lax.broadcasted_iota(jnp.int32, (chunk, chunk), 0)
    ci = lax.broadcasted_iota(jnp.int32, (chunk, chunk), 1)
    sign = 1 - 2 * d
    mask = (ri - ci) * sign >= 0
    mask_t = jnp.where((ci - ri) * sign >= 0, 1.0, 0.0).astype(BF16)
    acs_rows = _sel_dot(dta_t, mask_t)

    xs = xs_ref[0]
    rep = SSD_HEADS // SSD_GROUPS
    cb = []
    for g in range(SSD_GROUPS):
        cg = cm_ref[0, :, g * D_STATE:(g + 1) * D_STATE].astype(BF16)
        bg = bm_ref[0, :, g * D_STATE:(g + 1) * D_STATE].astype(BF16)
        cb.append((cg, bg, _dot_nt(cg, bg)))

    heads = range(SSD_HEADS)
    dta_rows = [dta_t[h:h + 1, :] for h in heads]
    acs_cols = [jnp.sum(jnp.where(mask, dta_rows[h], 0.0), axis=-1, keepdims=True) for h in heads]
    tots = [jnp.sum(dta_rows[h], axis=-1, keepdims=True) for h in heads]
    decays = [jnp.exp(jnp.where(mask, acs_cols[h] - acs_rows[h:h + 1, :], -jnp.inf)) for h in heads]
    xdts = [xs[:, h * SSD_HEAD_DIM:(h + 1) * SSD_HEAD_DIM] * dt[:, h:h + 1] for h in heads]
    states = [st_ref[h] for h in heads]
    y_diag = [_dot((cb[h // rep][2] * decays[h]).astype(BF16), xdts[h].astype(BF16)) for h in heads]
    y_off = [_dot_nt(cb[h // rep][0], states[h].astype(BF16)) for h in heads]
    upd = [_dot_tn((xdts[h] * jnp.exp(tots[h] - acs_cols[h])).astype(BF16), cb[h // rep][1]) for h in heads]
    for h in heads:
        st_ref[h] = states[h] * jnp.exp(tots[h]) + upd[h]
        y_ref[0, 0, :, h * SSD_HEAD_DIM:(h + 1) * SSD_HEAD_DIM] = y_diag[h] + jnp.exp(acs_cols[h]) * y_off[h]

    @pl.when(c == nc - 1)
    def _():
        sfin_ref[0, 0] = st_ref[...]


def _ssd_scan(xs, bm, cm, dt, a_log_l, s0=None, layer=0):
    bsz, t_len, xw = xs.shape
    chunk = CHUNK if t_len % CHUNK == 0 else t_len
    nc = t_len // chunk
    nh = SSD_HEADS
    bw = bm.shape[-1]
    dt4 = dt.reshape(bsz, t_len, 2, nh).transpose(0, 2, 1, 3)
    dtt = dt4.transpose(0, 1, 3, 2)

    def tok(b, d, c):
        return c + d * (nc - 1 - 2 * c)

    in_specs = [pl.BlockSpec((1, chunk, xw), lambda b, d, c: (b, tok(b, d, c), 0)),
                pl.BlockSpec((1, chunk, bw), lambda b, d, c: (b, tok(b, d, c), 0)),
                pl.BlockSpec((1, chunk, bw), lambda b, d, c: (b, tok(b, d, c), 0)),
                pl.BlockSpec((1, 1, chunk, nh), lambda b, d, c: (b, d, tok(b, d, c), 0)),
                pl.BlockSpec((1, 1, nh, chunk), lambda b, d, c: (b, d, 0, tok(b, d, c))),
                pl.BlockSpec((1, nh, 1), lambda b, d, c: (d, 0, 0))]
    args = [xs, bm, cm, dt4, dtt, a_log_l.reshape(2, nh, 1)]
    if s0 is not None:
        in_specs.append(pl.BlockSpec((1, 1, 1, nh, SSD_HEAD_DIM, D_STATE), lambda b, d, c: (b, layer, d, 0, 0, 0)))
        args.append(s0)
    return pl.pallas_call(
        functools.partial(_ssd_scan_kernel, has_s0=s0 is not None, nc=nc),
        grid=(bsz, 2, nc), in_specs=in_specs,
        out_specs=[pl.BlockSpec((1, 1, chunk, xw), lambda b, d, c: (d, b, tok(b, d, c), 0)),
                   pl.BlockSpec((1, 1, nh, SSD_HEAD_DIM, D_STATE), lambda b, d, c: (b, d, 0, 0, 0))],
        out_shape=[jax.ShapeDtypeStruct((2, bsz, t_len, xw), F32),
                   jax.ShapeDtypeStruct((bsz, 2, nh, SSD_HEAD_DIM, D_STATE), F32)],
        scratch_shapes=[pltpu.VMEM((nh, SSD_HEAD_DIM, D_STATE), F32)],
        compiler_params=_cparams("parallel", "arbitrary", "arbitrary"),
    )(*args)


def _ssd_post_kernel(yf_ref, yb_ref, xs_ref, z_ref, dsk_ref, g_ref, o_ref):
    z = z_ref[0]
    y = yf_ref[0, 0] + yb_ref[0, 0] + dsk_ref[...] * xs_ref[0]
    y = y * (z * jax.nn.sigmoid(z))
    y = y * lax.rsqrt(jnp.mean(y * y, axis=-1, keepdims=True) + EPS) * g_ref[...]
    o_ref[0] = y.astype(o_ref.dtype)


def _ssd_post(y2, xs, g3, d_skip, g_ssd):
    bsz, t_len, xw = xs.shape
    tt = _tile(t_len, 256)
    dsk = jnp.repeat(d_skip, SSD_HEAD_DIM).reshape(1, xw)
    return pl.pallas_call(
        _ssd_post_kernel, grid=(bsz, t_len // tt),
        in_specs=[pl.BlockSpec((1, 1, tt, xw), lambda b, t: (0, b, t, 0)),
                  pl.BlockSpec((1, 1, tt, xw), lambda b, t: (1, b, t, 0)),
                  pl.BlockSpec((1, tt, xw), lambda b, t: (b, t, 0)),
                  pl.BlockSpec((1, tt, xw), lambda b, t: (b, t, 0)),
                  pl.BlockSpec((1, xw), lambda b, t: (0, 0)),
                  pl.BlockSpec((1, xw), lambda b, t: (0, 0))],
        out_specs=pl.BlockSpec((1, tt, xw), lambda b, t: (b, t, 0)),
        out_shape=jax.ShapeDtypeStruct((bsz, t_len, xw), BF16),
        compiler_params=_cparams("parallel", "parallel"),
    )(y2, y2, xs, g3, dsk, g_ssd.reshape(1, xw))


def _merge_kernel(h_ref, wg0_ref, wg1_ref, wg2_ref, wg3_ref, ya_ref, yb_ref, yc_ref, yd_ref, wb_ref, o_ref):
    h = h_ref[0]
    acc = None
    wg_refs = (wg0_ref, wg1_ref, wg2_ref, wg3_ref)
    for i, y_ref in enumerate((ya_ref, yb_ref, yc_ref, yd_ref)):
        gate = jax.nn.sigmoid(_dot(h, wg_refs[i][...]))
        term = gate * _dot(y_ref[0], wb_ref[i].astype(BF16))
        acc = term if acc is None else acc + term
    o_ref[0] = acc.astype(o_ref.dtype)


def _merge(h, wg, ys, w_branch, layer):
    bsz, t_len, d = h.shape
    tm = _tile(t_len, 512)
    tn = _tile(d, 512)
    y_spec = pl.BlockSpec((1, tm, BRANCH_W), lambda n, b, t: (b, t, 0))
    npb = d // tn

    def wg_spec(i):
        return pl.BlockSpec((d, tn), lambda n, b, t: (0, i * npb + n))

    return pl.pallas_call(
        _merge_kernel, grid=(npb, bsz, t_len // tm),
        in_specs=[pl.BlockSpec((1, tm, d), lambda n, b, t: (b, t, 0)),
                  wg_spec(0), wg_spec(1), wg_spec(2), wg_spec(3),
                  y_spec, y_spec, y_spec, y_spec,
                  pl.BlockSpec((None, N_BRANCH, BRANCH_W, tn), lambda n, b, t: (layer, 0, 0, n))],
        out_specs=pl.BlockSpec((1, tm, tn), lambda n, b, t: (b, t, n)),
        out_shape=jax.ShapeDtypeStruct((bsz, t_len, d), BF16),
        compiler_params=_cparams("parallel", "parallel", "parallel"),
    )(h, wg, wg, wg, wg, *ys, w_branch)


def _moe_kernel(blk_e_ref, nused_ref, nvalid_ref, tok_ref, dst_ref, x_hbm, wg_ref, wu_ref, bg_ref, bu_ref,
                wd_ref, bd_ref, y_hbm, xbuf, xb, acc, gsem, ssem, *, ts, nf, nblk):
    i = pl.program_id(0)
    f = pl.program_id(1)
    nu = nused_ref[0]
    slot = i % 2

    def gather_rows(blk, s):
        base = blk * ts

        def body(r, carry):
            pltpu.make_async_copy(x_hbm.at[pl.ds(tok_ref[base + r], 1)], xbuf.at[s, pl.ds(r, 1)],
                                  gsem.at[s]).start()
            return carry

        lax.fori_loop(0, ts, body, 0, unroll=8)

    def scatter_rows(blk, s):
        base = blk * ts

        def start_row(r):
            pltpu.make_async_copy(acc.at[s, pl.ds(r, 1)], y_hbm.at[pl.ds(dst_ref[base + r], 1)],
                                  ssem.at[s]).start()

        def body8(g, carry):
            for u in range(8):
                start_row(g * 8 + u)
            return carry

        def body1(r, carry):
            start_row(r)
            return carry

        nv = nvalid_ref[blk]
        lax.fori_loop(0, nv // 8, body8, 0)
        lax.fori_loop(nv // 8 * 8, nv, body1, 0)

    def gather_wait(s):
        pltpu.make_async_copy(x_hbm.at[pl.ds(0, ts)], xbuf.at[s], gsem.at[s]).wait()

    def scatter_wait(blk, s):
        nv = nvalid_ref[blk]
        p = ts
        while p >= 1:
            @pl.when((nv & p) != 0)
            def _(p=p):
                pltpu.make_async_copy(acc.at[s, pl.ds(0, p)], y_hbm.at[pl.ds(0, p)], ssem.at[s]).wait()
            p //= 2

    @pl.when(i < nu)
    def _():
        @pl.when(f == 0)
        def _():
            @pl.when(i == 0)
            def _():
                gather_rows(0, 0)

            gather_wait(slot)
            xb[...] = xbuf[slot].astype(BF16)

            @pl.when(i + 1 < nu)
            def _():
                gather_rows(i + 1, 1 - slot)

        x = xb[...]
        g = _dot(x, wg_ref[...].astype(BF16)) + bg_ref[...]
        u = _dot(x, wu_ref[...].astype(BF16)) + bu_ref[...]
        g = jnp.minimum(g, SWIGLU_LIMIT)
        u = jnp.clip(u, -SWIGLU_LIMIT, SWIGLU_LIMIT)
        act = g * jax.nn.sigmoid(SWIGLU_ALPHA * g) * (u + 1.0)
        part = _dot(act.astype(BF16), wd_ref[...].astype(BF16))

        @pl.when(f == 0)
        def _():
            acc[slot] = part + bd_ref[...]

        @pl.when(f > 0)
        def _():
            acc[slot] += part

        @pl.when(f == nf - 1)
        def _():
            @pl.when(i >= 1)
            def _():
                scatter_wait(i - 1, 1 - slot)

            scatter_rows(i, slot)

    @pl.when((i == nblk - 1) & (f == nf - 1) & (nu >= 1))
    def _():
        scatter_wait(nu - 1, (nu - 1) % 2)


def _moe_blocks(h2, blk_e, nused, nvalid, slot_tok, slot_dst, n_rows_out, w_gu, b_gu, w_dn, b_dn, layer):
    d = h2.shape[1]
    n_exp, _, ff2 = w_gu.shape[1:]
    ff = ff2 // 2
    ts, fc = MOE_TOKENS, _tile(ff, MOE_FF_CHUNK)
    nf = ff // fc
    nblk = slot_tok.shape[0] // ts

    def fcl(i, f, nu):
        return jnp.where(i < nu[0], f, nf - 1)

    grid_spec = pltpu.PrefetchScalarGridSpec(
        num_scalar_prefetch=5, grid=(nblk, nf),
        in_specs=[
            pl.BlockSpec(memory_space=pl.ANY),
            pl.BlockSpec((None, None, d, fc), lambda i, f, be, nu, *_: (layer, be[i], 0, fcl(i, f, nu))),
            pl.BlockSpec((None, None, d, fc), lambda i, f, be, nu, *_: (layer, be[i], 0, nf + fcl(i, f, nu))),
            pl.BlockSpec((None, None, 1, fc), lambda i, f, be, nu, *_: (layer, be[i], 0, fcl(i, f, nu))),
            pl.BlockSpec((None, None, 1, fc), lambda i, f, be, nu, *_: (layer, be[i], 0, nf + fcl(i, f, nu))),
            pl.BlockSpec((None, None, fc, d), lambda i, f, be, nu, *_: (layer, be[i], fcl(i, f, nu), 0)),
            pl.BlockSpec((None, None, 1, d), lambda i, f, be, nu, *_: (layer, be[i], 0, 0)),
        ],
        out_specs=pl.BlockSpec(memory_space=pl.ANY),
        scratch_shapes=[pltpu.VMEM((2, ts, d), F32), pltpu.VMEM((ts, d), BF16), pltpu.VMEM((2, ts, d), F32),
                        pltpu.SemaphoreType.DMA((2,)), pltpu.SemaphoreType.DMA((2,))],
    )
    depth = w_gu.shape[0]
    return pl.pallas_call(
        functools.partial(_moe_kernel, ts=ts, nf=nf, nblk=nblk), grid_spec=grid_spec,
        out_shape=jax.ShapeDtypeStruct((n_rows_out, d), F32),
        compiler_params=_cparams("arbitrary", "arbitrary"),
    )(blk_e, nused, nvalid, slot_tok, slot_dst, h2, w_gu, w_gu, b_gu.reshape(depth, n_exp, 1, ff2),
      b_gu.reshape(depth, n_exp, 1, ff2), w_dn, b_dn.reshape(depth, n_exp, 1, d))


def _moe(h2, logits, w_gu, b_gu, w_dn, b_dn, layer):
    n_tok, d = h2.shape
    n_exp = logits.shape[1]
    ts = MOE_TOKENS
    top_v, top_i = lax.top_k(logits, TOP_K)
    gate_w = jax.nn.softmax(top_v, axis=-1)
    n_asg = n_tok * TOP_K
    flat_e = top_i.reshape(-1)
    onehot = (flat_e[:, None] == jnp.arange(n_exp, dtype=flat_e.dtype)[None, :]).astype(jnp.int32)
    counts = jnp.sum(onehot, axis=0)
    padded = (counts + ts - 1) // ts * ts
    pad_ends = jnp.cumsum(padded)
    pad_starts = pad_ends - padded
    pos = jnp.sum(onehot * (jnp.cumsum(onehot, axis=0) - 1 + pad_starts[None, :]), axis=1).astype(jnp.int32)
    nblk = -(-n_asg // ts) + n_exp
    cap = nblk * ts
    a = jnp.arange(n_asg, dtype=jnp.int32)
    slot_tok = jnp.zeros((cap,), jnp.int32).at[pos].set(a // TOP_K)
    slot_dst = jnp.zeros((cap,), jnp.int32).at[pos].set((a % TOP_K) * n_tok + a // TOP_K)
    blk_start = jnp.arange(nblk, dtype=jnp.int32)[:, None] * ts
    nvalid = jnp.sum(jnp.clip(pad_starts[None, :] + counts[None, :] - blk_start, 0, ts)
                     * ((pad_starts[None, :] <= blk_start) & (blk_start < pad_ends[None, :])), axis=1).astype(jnp.int32)
    nused = (pad_ends[-1] // ts).astype(jnp.int32)
    starts = jnp.arange(nblk, dtype=jnp.int32) * ts
    blk_e = jnp.sum((pad_ends[None, :] <= starts[:, None]).astype(jnp.int32), axis=1)
    blk_e = jnp.minimum(blk_e, n_exp - 1)
    last_e = jnp.sum(jnp.where(jnp.arange(nblk) == nused - 1, blk_e, 0))
    blk_e = jnp.where(jnp.arange(nblk) < nused, blk_e, last_e).astype(jnp.int32)
    y4 = _moe_blocks(h2, blk_e, nused.reshape(1), nvalid, slot_tok, slot_dst, n_asg, w_gu, b_gu, w_dn, b_dn, layer)
    return y4, gate_w


def _combine_kernel(x_ref, g_ref, w_ref, y0_ref, y1_ref, y2_ref, y3_ref, o_ref):
    w = w_ref[...]
    m = None
    for j, y_ref in enumerate((y0_ref, y1_ref, y2_ref, y3_ref)):
        term = w[:, j:j + 1] * y_ref[...]
        m = term if m is None else m + term
    o_ref[0] = x_ref[0] + g_ref[0] * m


def _moe_combine(x, gate, y4, gate_w, row0):
    bsz, t_len, d = x.shape
    n_tok = gate_w.shape[0]
    tt = _tile(t_len, 256)
    tpb = t_len // tt
    t0 = row0 // tt
    nt = n_tok // tt

    def y_spec(j):
        return pl.BlockSpec((tt, d), lambda b, t: (j * nt + t0 + b * tpb + t, 0))

    if gate.shape[0] == 1:
        g_spec = pl.BlockSpec((1, 1, d), lambda b, t: (0, 0, 0))
    else:
        g_spec = pl.BlockSpec((1, 1, d), lambda b, t: (b, 0, 0))
    return pl.pallas_call(
        _combine_kernel, grid=(bsz, tpb),
        in_specs=[pl.BlockSpec((1, tt, d), lambda b, t: (b, t, 0)), g_spec,
                  pl.BlockSpec((tt, TOP_K), lambda b, t: (t0 + b * tpb + t, 0)),
                  y_spec(0), y_spec(1), y_spec(2), y_spec(3)],
        out_specs=pl.BlockSpec((1, tt, d), lambda b, t: (b, t, 0)),
        out_shape=jax.ShapeDtypeStruct((bsz, t_len, d), F32),
        compiler_params=_cparams("parallel", "parallel"),
    )(x, gate, gate_w, y4, y4, y4, y4)


def _token_mixers(h, lw, layer, lam_init, ctx):
    latent = ctx is not None
    bsz, t_len, _ = h.shape
    qkv = _mm(h, lw['w_qkv'])
    g2 = _mm(h, lw['w_lat'])
    g3 = _mm(h, lw['w_ssd'])
    rope = _rope_tables(t_len, BRANCH_W) if latent else None

    qa = _headnorm(qkv, 0, lw['g_na_q'], HEAD_DIM)
    ka = _headnorm(qkv, 1, lw['g_na_k'], HEAD_DIM)
    if latent:
        ya = _na_latent(qa, ka, qkv, 2, ctx['na_k'], ctx['na_v'], layer, lw['rpb'])
    else:
        ya = _attention(qa, ka, qkv, 2, heads=NA_HEADS, maps=1, pieces=_na_pieces, dv=HEAD_DIM,
                        scale=HEAD_DIM ** -0.5)

    qb = _headnorm(qkv, 3, lw['g_df_q'], HEAD_DIM, rope)
    kb = _headnorm(qkv, 4, lw['g_df_k'], HEAD_DIM, rope)
    diff = (lw['lam_q1'], lw['lam_k1'], lw['lam_q2'], lw['lam_k2'], lw['g_df_sub'], lam_init)
    df_ctx = (ctx['df_k'], ctx['df_v'], layer, 0) if latent else None
    yb = _attention(qb, kb, qkv, 5, heads=DF_HEADS, maps=2, pieces=_df_pieces, dv=2 * HEAD_DIM,
                    scale=HEAD_DIM ** -0.5, ctx=df_ctx, diff=diff)

    consts = _mla_consts()
    qc, kc, vc, ckv, krope = _mla_prep(g2, lw['g_mla_cq'], lw['g_mla_ckv'], lw['w_uq'], lw['w_ukv'],
                                       lw['g_mla_q'], lw['g_mla_k'], consts, rope)
    mla_ctx = None
    if latent:
        kcx, vcx = _mla_ctx(ctx['mla_ckv'], ctx['mla_krope'], layer, lw['w_ukv'], lw['g_mla_k'], consts)
        mla_ctx = (kcx, vcx, 0, 0)
    yc = _attention(qc, kc, vc, 0, heads=MLA_HEADS, maps=1, pieces=_mla_pieces, dv=V_DIM,
                    scale=MLA_QK ** -0.5, ctx=mla_ctx)

    xs, bm, cm, dt = _ssd_prep(g3, lw['conv_w'], lw['conv_b'], lw['dt_bias'])
    y2, s_fin = _ssd_scan(xs, bm, cm, dt, lw['a_log'], ctx['ssd'] if latent else None, layer)
    yd = _ssd_post(y2, xs, g3, lw['d_skip'], lw['g_ssd'])

    merged = _merge(h, lw['w_gate'], (ya, yb, yc, yd), lw['w_branch'], layer)
    if latent:
        return merged, None
    new = dict(
        na_k=ka.reshape(bsz, t_len, NA_HEADS, HEAD_DIM),
        na_v=qkv[..., 2 * BRANCH_W:3 * BRANCH_W].reshape(bsz, t_len, NA_HEADS, HEAD_DIM),
        df_k=kb.reshape(bsz, t_len, DF_HEADS, 2, HEAD_DIM),
        df_v=qkv[..., 5 * BRANCH_W:6 * BRANCH_W].reshape(bsz, t_len, DF_HEADS, 2 * HEAD_DIM),
        mla_ckv=ckv, mla_krope=krope, ssd=s_fin)
    return merged, new


def kernel(x_prompt, x_sample, cache_na_k, cache_na_v, cache_df_k, cache_df_v, cache_mla_ckv, cache_mla_krope, state_ssd, c, c_ctx, w_mod, b_mod, g_norm1, g_norm2, w_in, g_na_q, g_na_k, rpb, g_df_q, g_df_k, lam_q1, lam_k1, lam_q2, lam_k2, g_df_sub, g_mla_cq, g_mla_ckv, w_uq, w_ukv, g_mla_q, g_mla_k, conv_w, conv_b, dt_bias, a_log, d_skip, g_ssd, w_branch, w_o, w_router, b_router, w_gu, b_gu, w_dn, b_dn):
    xp, xs = x_prompt, x_sample
    depth, d, _ = w_in.shape
    dec_b, _, past = cache_na_k.shape[:3]
    n_p = xp.shape[0] * xp.shape[1]
    n_exp = w_router.shape[-1]

    n_rows = -(-(dec_b + 1) // 8) * 8
    cc = jnp.zeros((n_rows, d), F32).at[:dec_b].set(c).at[dec_b].set(c_ctx)
    mods = _mod_params(cc, w_mod, b_mod)

    ctx_all = dict(
        na_k=cache_na_k.reshape(dec_b, depth, past, NA_HEADS * HEAD_DIM),
        na_v=cache_na_v.reshape(dec_b, depth, past, NA_HEADS * HEAD_DIM),
        df_k=cache_df_k.reshape(dec_b, depth, past, DF_HEADS * 2 * HEAD_DIM),
        df_v=cache_df_v.reshape(dec_b, depth, past, DF_HEADS * 2 * HEAD_DIM),
        mla_ckv=cache_mla_ckv, mla_krope=cache_mla_krope, ssd=state_ssd)

    o_lat = 6 * BRANCH_W
    o_ssd = o_lat + Q_RANK + KV_RANK + ROPE_DIM
    o_gate = o_ssd + BRANCH_W + conv_w.shape[-1] + 2 * SSD_HEADS
    ne_pad = -(-n_exp // LANE) * LANE

    names = ('na_k', 'na_v', 'df_k', 'df_v', 'mla_ckv', 'mla_krope', 'ssd')
    new = {n: [] for n in names}
    for l in range(depth):
        lam_init = 0.8 - 0.6 * math.exp(-0.3 * l)
        wq, wkv, gq, gk = _mla_perm_weights(w_uq[l], w_ukv[l], g_mla_q[l], g_mla_k[l])
        w_l = w_in[l]
        lw = dict(
            w_qkv=w_l[:, :o_lat].astype(BF16), w_lat=w_l[:, o_lat:o_ssd].astype(BF16),
            w_ssd=w_l[:, o_ssd:o_gate].astype(BF16),
            w_gate=w_l[:, o_gate:].astype(BF16),
            g_na_q=g_na_q[l], g_na_k=g_na_k[l], rpb=rpb[l], g_df_q=g_df_q[l], g_df_k=g_df_k[l],
            lam_q1=lam_q1[l], lam_k1=lam_k1[l], lam_q2=lam_q2[l], lam_k2=lam_k2[l], g_df_sub=g_df_sub[l],
            g_mla_cq=g_mla_cq[l], g_mla_ckv=g_mla_ckv[l], w_uq=wq, w_ukv=wkv, g_mla_q=gq, g_mla_k=gk,
            conv_w=conv_w[l], conv_b=conv_b[l], dt_bias=dt_bias[l], a_log=a_log[l], d_skip=d_skip[l],
            g_ssd=g_ssd[l], w_branch=w_branch)
        mc = [m[:, None, :] for m in jnp.split(mods[l, :dec_b], 6, axis=-1)]
        mx = [m[:, None, :] for m in jnp.split(mods[l, dec_b:dec_b + 1], 6, axis=-1)]

        hp = _norm_mod(xp, g_norm1[l], mx[0], mx[1])
        mp, ctx_new = _token_mixers(hp, lw, l, lam_init, None)
        xp = _mm_residual(mp, w_o, l, xp, mx[2])
        hs = _norm_mod(xs, g_norm1[l], mc[0], mc[1])
        ms, _ = _token_mixers(hs, lw, l, lam_init, ctx_all)
        xs = _mm_residual(ms, w_o, l, xs, mc[2])

        wr = jnp.zeros((d, ne_pad), F32).at[:, :n_exp].set(w_router[l])
        br = jnp.zeros((1, ne_pad), F32).at[0, :n_exp].set(b_router[l])
        hp2, lg_p = _norm_mod(xp, g_norm2[l], mx[3], mx[4], router=(wr, br))
        hs2, lg_s = _norm_mod(xs, g_norm2[l], mc[3], mc[4], router=(wr, br))
        h2 = jnp.concatenate([hp2.reshape(-1, d), hs2.reshape(-1, d)], axis=0)
        logits = jnp.concatenate([lg_p.reshape(-1, ne_pad), lg_s.reshape(-1, ne_pad)], axis=0)[:, :n_exp]
        y4, gate_w = _moe(h2, logits, w_gu, b_gu, w_dn, b_dn, l)
        xp = _moe_combine(xp, mx[5], y4, gate_w, 0)
        xs = _moe_combine(xs, mc[5], y4, gate_w, n_p)
        for n in names:
            new[n].append(ctx_new[n])
    return (xp, xs) + tuple(jnp.stack(new[n], axis=1) for n in names)
```

```python
import functools
import math

import jax
import jax.numpy as jnp
from jax import lax
import numpy as np
from jax.experimental import pallas as pl
from jax.experimental.pallas import tpu as pltpu

F32 = jnp.float32
BF16 = jnp.bfloat16

GRID_W = 64
HEAD_DIM = 64
ROPE_BASE = 10000.0
EPS = 1e-6
NA_HEADS = 8
WIN_R = 8
WIN_C = 16
DF_HEADS = 4
MLA_HEADS = 4
Q_RANK = 384
KV_RANK = 128
NOPE_DIM = 128
ROPE_DIM = 64
V_DIM = 128
MLA_QK = NOPE_DIM + ROPE_DIM
SSD_HEADS = 8
SSD_HEAD_DIM = 64
SSD_GROUPS = 2
D_STATE = 128
CHUNK = 128
TOP_K = 4
SWIGLU_ALPHA = 1.702
SWIGLU_LIMIT = 7.0
N_BRANCH = 4
BRANCH_W = 512

V7X_VMEM_BYTES = 64 * 1024 * 1024
VMEM_LIMIT = V7X_VMEM_BYTES - 8 * 1024 * 1024
LANE = 128

NA_ROWS_PER_STEP = 4
MOE_TOKENS = 512
MOE_FF_CHUNK = 512


def _cparams(*sem):
    return pltpu.CompilerParams(dimension_semantics=sem, vmem_limit_bytes=VMEM_LIMIT)


def _tile(n, pref):
    t = min(n, pref)
    while n % t:
        t //= 2
    return t


def _dot(a, b):
    return jnp.dot(a, b, preferred_element_type=F32)


def _dot_nt(a, b):
    return lax.dot_general(a, b, (((1,), (1,)), ((), ())), preferred_element_type=F32)


def _dot_tn(a, b):
    return lax.dot_general(a, b, (((0,), (0,)), ((), ())), preferred_element_type=F32)


def _split3(x):
    h = x.astype(BF16)
    r = x - h.astype(F32)
    m = r.astype(BF16)
    lo = (r - m.astype(F32)).astype(BF16)
    return h, m, lo


def _sel_dot(x, sel):
    h, m, lo = _split3(x)
    return _dot(h, sel) + _dot(m, sel) + _dot(lo, sel)


def _dot_hi(a, b):
    a1, a2, _ = _split3(a)
    b1, b2, _ = _split3(b)
    return _dot(a1, b1) + (_dot(a1, b2) + _dot(a2, b1))


def _rope_rotate(y, cos, sin):
    w = y.shape[-1]
    lane = lax.broadcasted_iota(jnp.int32, y.shape, 1)
    partner = jnp.where((lane % 32) < 16, pltpu.roll(y, w - 16, 1), pltpu.roll(y, 16, 1))
    return y * cos + partner * sin


def _rope_tables(t_len, width):
    half = HEAD_DIM // 2
    inv = jnp.asarray(ROPE_BASE ** (-np.arange(0, half, 2) / half), F32)
    t = np.arange(t_len)
    cols = []
    sins = []
    for pos in (t // GRID_W, t % GRID_W):
        ang = jnp.asarray(pos, F32)[:, None] * inv[None, :]
        c, s = jnp.cos(ang), jnp.sin(ang)
        cols += [c, c]
        sins += [-s, s]
    cos = jnp.concatenate(cols, axis=-1)
    sin = jnp.concatenate(sins, axis=-1)
    rep = width // HEAD_DIM
    return jnp.tile(cos, (1, rep)), jnp.tile(sin, (1, rep))


def _group_selector(width, group):
    g = np.arange(width) // group
    return jnp.asarray(g[:, None] == g[None, :], BF16)


def _mod_kernel(c_ref, w_ref, b_ref, o_ref):
    c = c_ref[...]
    s = (c * jax.nn.sigmoid(c)).astype(BF16)
    o_ref[0] = _dot(s, w_ref[0].astype(BF16)) + b_ref[0]


def _mod_params(cc, w_mod, b_mod):
    depth, d, n = w_mod.shape
    rows = cc.shape[0]
    tn = _tile(n, 1024)
    return pl.pallas_call(
        _mod_kernel,
        grid=(depth, n // tn),
        in_specs=[pl.BlockSpec((rows, d), lambda l, j: (0, 0)),
                  pl.BlockSpec((1, d, tn), lambda l, j: (l, 0, j)),
                  pl.BlockSpec((1, 1, tn), lambda l, j: (l, 0, j))],
        out_specs=pl.BlockSpec((1, rows, tn), lambda l, j: (l, 0, j)),
        out_shape=jax.ShapeDtypeStruct((depth, rows, n), F32),
        compiler_params=_cparams("parallel", "parallel"),
    )(cc, w_mod, b_mod.reshape(depth, 1, n))


def _norm_mod_body(x_ref, g_ref, sh_ref, sc_ref):
    x = x_ref[0]
    y = x * lax.rsqrt(jnp.mean(x * x, axis=-1, keepdims=True) + EPS)
    y = y * g_ref[...]
    return y * (1.0 + sc_ref[0]) + sh_ref[0]


def _norm_mod_kernel(x_ref, g_ref, sh_ref, sc_ref, o_ref):
    o_ref[0] = _norm_mod_body(x_ref, g_ref, sh_ref, sc_ref).astype(o_ref.dtype)


def _norm_mod_router_kernel(x_ref, g_ref, sh_ref, sc_ref, wr_ref, br_ref, o_ref, lg_ref):
    y = _norm_mod_body(x_ref, g_ref, sh_ref, sc_ref)
    o_ref[0] = y.astype(o_ref.dtype)
    lg_ref[0] = _dot_hi(y, wr_ref[...]) + br_ref[...]


def _mod_spec(m, d):
    if m.shape[0] == 1:
        return pl.BlockSpec((1, 1, d), lambda b, t: (0, 0, 0))
    return pl.BlockSpec((1, 1, d), lambda b, t: (b, 0, 0))


def _norm_mod(x, g, shift, scale, router=None):
    bsz, t_len, d = x.shape
    tt = _tile(t_len, 256)
    in_specs = [pl.BlockSpec((1, tt, d), lambda b, t: (b, t, 0)),
                pl.BlockSpec((1, d), lambda b, t: (0, 0)),
                _mod_spec(shift, d), _mod_spec(scale, d)]
    out_spec = pl.BlockSpec((1, tt, d), lambda b, t: (b, t, 0))
    out_shape = jax.ShapeDtypeStruct((bsz, t_len, d), BF16 if router is None else F32)
    args = [x, g.reshape(1, d), shift, scale]
    if router is None:
        return pl.pallas_call(
            _norm_mod_kernel, grid=(bsz, t_len // tt), in_specs=in_specs, out_specs=out_spec,
            out_shape=out_shape, compiler_params=_cparams("parallel", "parallel"))(*args)
    wr, br = router
    ne = wr.shape[1]
    in_specs += [pl.BlockSpec((d, ne), lambda b, t: (0, 0)), pl.BlockSpec((1, ne), lambda b, t: (0, 0))]
    return pl.pallas_call(
        _norm_mod_router_kernel, grid=(bsz, t_len // tt), in_specs=in_specs,
        out_specs=[out_spec, pl.BlockSpec((1, tt, ne), lambda b, t: (b, t, 0))],
        out_shape=[out_shape, jax.ShapeDtypeStruct((bsz, t_len, ne), F32)],
        compiler_params=_cparams("parallel", "parallel"))(*args, wr, br)


def _mm_kernel(x_ref, w_ref, o_ref):
    o_ref[0] = _dot(x_ref[0].astype(BF16), w_ref[...].astype(BF16)).astype(o_ref.dtype)


def _mm_res_kernel(x_ref, w_ref, r_ref, g_ref, o_ref):
    o_ref[0] = r_ref[0] + g_ref[0] * _dot(x_ref[0].astype(BF16), w_ref[...].astype(BF16))


def _w_spec(w, layer, k, tn):
    if w.ndim == 3:
        return pl.BlockSpec((None, k, tn), lambda n, b, t: (layer, 0, n))
    return pl.BlockSpec((k, tn), lambda n, b, t: (0, n))


def _mm(x, w, layer=0, out_dtype=F32, tm_pref=1024, tn_pref=512):
    bsz, t_len, k = x.shape
    n = w.shape[-1]
    tm = _tile(t_len, tm_pref)
    tn = n if n % LANE else _tile(n, tn_pref)
    return pl.pallas_call(
        _mm_kernel,
        grid=(n // tn, bsz, t_len // tm),
        in_specs=[pl.BlockSpec((1, tm, k), lambda n_, b, t: (b, t, 0)), _w_spec(w, layer, k, tn)],
        out_specs=pl.BlockSpec((1, tm, tn), lambda n_, b, t: (b, t, n_)),
        out_shape=jax.ShapeDtypeStruct((bsz, t_len, n), out_dtype),
        compiler_params=_cparams("parallel", "parallel", "parallel"),
    )(x, w)


def _mm_residual(x, w, layer, res, gate):
    bsz, t_len, k = x.shape
    n = w.shape[-1]
    tm = _tile(t_len, 1024)
    tn = _tile(n, 512)
    if gate.shape[0] == 1:
        g_spec = pl.BlockSpec((1, 1, tn), lambda n_, b, t: (0, 0, n_))
    else:
        g_spec = pl.BlockSpec((1, 1, tn), lambda n_, b, t: (b, 0, n_))
    return pl.pallas_call(
        _mm_res_kernel,
        grid=(n // tn, bsz, t_len // tm),
        in_specs=[pl.BlockSpec((1, tm, k), lambda n_, b, t: (b, t, 0)), _w_spec(w, layer, k, tn),
                  pl.BlockSpec((1, tm, tn), lambda n_, b, t: (b, t, n_)), g_spec],
        out_specs=pl.BlockSpec((1, tm, tn), lambda n_, b, t: (b, t, n_)),
        out_shape=jax.ShapeDtypeStruct((bsz, t_len, n), F32),
        compiler_params=_cparams("parallel", "parallel", "parallel"),
    )(x, w, res, gate)


QKV_GROUPS = 6
QKV_VALUE_GROUPS = (2, 5)
QKV_ROPE_GROUPS = (3, 4)


def _qkv_kernel(*refs, rope):
    if rope:
        x_ref, w_ref, g_ref, e_ref, cos_ref, sin_ref, o_ref = refs
    else:
        x_ref, w_ref, g_ref, e_ref, o_ref = refs
    n = pl.program_id(0)
    y = _dot(x_ref[0], w_ref[...])
    is_value = functools.reduce(jnp.logical_or, [n == v for v in QKV_VALUE_GROUPS])

    @pl.when(is_value)
    def _():
        o_ref[0] = y

    @pl.when(jnp.logical_not(is_value))
    def _():
        ms = _sel_dot(y * y, e_ref[...]) * (1.0 / HEAD_DIM)
        yn = y * lax.rsqrt(ms + EPS) * g_ref[0]
        if rope:
            is_rope = functools.reduce(jnp.logical_or, [n == v for v in QKV_ROPE_GROUPS])

            @pl.when(is_rope)
            def _():
                o_ref[0] = _rope_rotate(yn, cos_ref[...], sin_ref[...])

            @pl.when(jnp.logical_not(is_rope))
            def _():
                o_ref[0] = yn
        else:
            o_ref[0] = yn


def _qkv_proj(h, w_qkv, gains, rope=None):
    bsz, t_len, k = h.shape
    w = BRANCH_W
    tm = _tile(t_len, 1024)
    in_specs = [pl.BlockSpec((1, tm, k), lambda n, b, t: (b, t, 0)),
                pl.BlockSpec((k, w), lambda n, b, t: (0, n)),
                pl.BlockSpec((1, 1, w), lambda n, b, t: (n, 0, 0)),
                pl.BlockSpec((w, w), lambda n, b, t: (0, 0))]
    args = [h, w_qkv, gains, _group_selector(w, HEAD_DIM)]
    if rope is not None:
        in_specs += [pl.BlockSpec((tm, w), lambda n, b, t: (t, 0))] * 2
        args += list(rope)
    return pl.pallas_call(
        functools.partial(_qkv_kernel, rope=rope is not None),
        grid=(QKV_GROUPS, bsz, t_len // tm), in_specs=in_specs,
        out_specs=pl.BlockSpec((1, tm, w), lambda n, b, t: (b, t, n)),
        out_shape=jax.ShapeDtypeStruct((bsz, t_len, QKV_GROUPS * w), F32),
        compiler_params=_cparams("parallel", "parallel", "parallel"),
    )(*args)


def _attn_kernel(*refs, heads, maps, pieces, dv, scale, has_ctx, diff, post_scale):
    it = iter(refs)
    q_ref, k_ref, v_ref = next(it), next(it), next(it)
    kc_ref = vc_ref = None
    if has_ctx:
        kc_ref, vc_ref = next(it), next(it)
    if diff is not None:
        lq1, lk1, lq2, lk2, gsub_ref = next(it), next(it), next(it), next(it), next(it)
    o_ref = next(it)

    q = q_ref[0].astype(BF16)
    k = k_ref[0].astype(BF16)
    v = v_ref[0].astype(BF16)
    if has_ctx:
        kc = kc_ref[0, 0].astype(BF16)
        vc = vc_ref[0, 0].astype(BF16)
    if diff is not None:
        lam = (jnp.exp(jnp.sum(lq1[...] * lk1[...], axis=-1, keepdims=True))
               - jnp.exp(jnp.sum(lq2[...] * lk2[...], axis=-1, keepdims=True)) + diff)

    for h in range(heads):
        w_own = w_ctx = None
        for m in range(maps):
            s = s_c = None
            for (qo, ko, d) in pieces(h, m):
                part = _dot_nt(q[:, qo:qo + d], k[:, ko:ko + d])
                s = part if s is None else s + part
                if has_ctx:
                    part_c = _dot_nt(q[:, qo:qo + d], kc[:, ko:ko + d])
                    s_c = part_c if s_c is None else s_c + part_c
            s = s * scale
            mx = jnp.max(s, axis=-1, keepdims=True)
            if has_ctx:
                s_c = s_c * scale
                mx = jnp.maximum(mx, jnp.max(s_c, axis=-1, keepdims=True))
            p = jnp.exp(s - mx)
            den = jnp.sum(p, axis=-1, keepdims=True)
            if has_ctx:
                p_c = jnp.exp(s_c - mx)
                den = den + jnp.sum(p_c, axis=-1, keepdims=True)
            inv = 1.0 / den
            if m == 1:
                inv = -lam * inv
            w_own = p * inv if w_own is None else w_own + p * inv
            if has_ctx:
                w_ctx = p_c * inv if w_ctx is None else w_ctx + p_c * inv
        o = _dot(w_own.astype(BF16), v[:, h * dv:(h + 1) * dv])
        if has_ctx:
            o = o + _dot(w_ctx.astype(BF16), vc[:, h * dv:(h + 1) * dv])
        if diff is not None:
            o = o * lax.rsqrt(jnp.mean(o * o, axis=-1, keepdims=True) + EPS) * gsub_ref[...] * post_scale
        o_ref[0, :, h * dv:(h + 1) * dv] = o.astype(o_ref.dtype)


def _attention(q, k, v, v_col, *, heads, maps, pieces, dv, scale, ctx=None, diff=None, q_col=0, k_col=0,
               qk_width=None):
    bsz, t_len, _ = q.shape
    wq = wk = qk_width if qk_width is not None else q.shape[-1]
    wv = heads * dv
    tq = _tile(t_len, 256)
    in_specs = [pl.BlockSpec((1, tq, wq), lambda b, t: (b, t, q_col)),
                pl.BlockSpec((1, t_len, wk), lambda b, t: (b, 0, k_col)),
                pl.BlockSpec((1, t_len, wv), lambda b, t: (b, 0, v_col))]
    args = [q, k, v]
    if ctx is not None:
        kc, vc, layer, vc_col = ctx
        tc = kc.shape[2]
        in_specs += [pl.BlockSpec((1, 1, tc, wk), lambda b, t: (b, layer, 0, 0)),
                     pl.BlockSpec((1, 1, tc, wv), lambda b, t: (b, layer, 0, vc_col))]
        args += [kc, vc]
    lam_init = 0.0
    if diff is not None:
        lq1, lk1, lq2, lk2, gsub, lam_init = diff
        for a in (lq1, lk1, lq2, lk2, gsub):
            in_specs.append(pl.BlockSpec((1, a.shape[0]), lambda b, t: (0, 0)))
            args.append(a.reshape(1, -1))
    kern = functools.partial(
        _attn_kernel, heads=heads, maps=maps, pieces=pieces, dv=dv, scale=scale,
        has_ctx=ctx is not None, diff=lam_init if diff is not None else None,
        post_scale=1.0 - lam_init)
    return pl.pallas_call(
        kern, grid=(bsz, t_len // tq), in_specs=in_specs,
        out_specs=pl.BlockSpec((1, tq, wv), lambda b, t: (b, t, 0)),
        out_shape=jax.ShapeDtypeStruct((bsz, t_len, wv), BF16),
        compiler_params=_cparams("parallel", "parallel"),
    )(*args)


def _na_pieces(h, m):
    return [(h * HEAD_DIM, h * HEAD_DIM, HEAD_DIM)]


def _df_pieces(h, m):
    o = (2 * h + m) * HEAD_DIM
    return [(o, o, HEAD_DIM)]


def _mla_pieces(h, m):
    nope_w = MLA_HEADS * NOPE_DIM
    return [(h * NOPE_DIM, h * NOPE_DIM, NOPE_DIM),
            (nope_w + h * ROPE_DIM, nope_w + h * ROPE_DIM, ROPE_DIM)]


def _na_latent_kernel(q_ref, k_ref, v_ref, kc_ref, vc_ref, bias_ref, o_ref, *, rows, kr, rps, scale):
    kc = kc_ref[0, 0].astype(BF16)
    vc = vc_ref[0, 0].astype(BF16)
    for rr in range(rps):
        r = pl.program_id(1) * rps + rr
        start_row = jnp.clip(r - WIN_R // 2, 0, rows - kr)
        start = pl.multiple_of(start_row * GRID_W, GRID_W)
        q = q_ref[0, rr * GRID_W:(rr + 1) * GRID_W, :].astype(BF16)
        kl = k_ref[0, pl.ds(start, kr * GRID_W), :].astype(BF16)
        vl = v_ref[0, pl.ds(start, kr * GRID_W), :].astype(BF16)
        ro0 = start_row - r + WIN_R - 1
        s_heads = []
        for h in range(NA_HEADS):
            sl = slice(h * HEAD_DIM, (h + 1) * HEAD_DIM)
            bias = jnp.concatenate([bias_ref[h, ro0 + 2 * j] for j in range(kr // 2)], axis=-1)
            s_loc = _dot_nt(q[:, sl], kl[:, sl]) * scale + bias
            s_ctx = _dot_nt(q[:, sl], kc[:, sl]) * scale
            s_heads.append(jnp.concatenate([s_loc, s_ctx], axis=-1))
        s = jnp.concatenate(s_heads, axis=0)
        p = jnp.exp(s - jnp.max(s, axis=-1, keepdims=True))
        pn = (p * (1.0 / jnp.sum(p, axis=-1, keepdims=True))).astype(BF16)
        n_loc = kr * GRID_W
        for h in range(NA_HEADS):
            sl = slice(h * HEAD_DIM, (h + 1) * HEAD_DIM)
            ph = pn[h * GRID_W:(h + 1) * GRID_W]
            o = _dot(ph[:, :n_loc], vl[:, sl]) + _dot(ph[:, n_loc:], vc[:, sl])
            o_ref[0, rr * GRID_W:(rr + 1) * GRID_W, sl] = o.astype(o_ref.dtype)


def _na_bias_table(rpb_l):
    qc = np.arange(GRID_W)
    kc = np.arange(GRID_W)
    qstart = np.clip(qc - WIN_C // 2, 0, GRID_W - WIN_C)
    valid = (kc[None, :] >= qstart[:, None]) & (kc[None, :] < qstart[:, None] + WIN_C)
    c_off = np.clip(kc[None, :] - qc[:, None] + WIN_C - 1, 0, 2 * WIN_C - 2)
    onehot = (c_off[None] == np.arange(2 * WIN_C - 1)[:, None, None]) & valid[None]
    tab = jnp.einsum('hrc,cqk->hrqk', rpb_l.astype(F32), jnp.asarray(onehot, F32),
                     precision=lax.Precision.HIGHEST)
    tab = jnp.where(valid[None, None], tab, -jnp.inf)
    return jnp.concatenate([tab[:, :-1], tab[:, 1:]], axis=-1)


def _na_latent(q, k, v, v_col, kc, vc, layer, rpb_l, q_col=0, k_col=0):
    bsz, t_len, _ = q.shape
    w = NA_HEADS * HEAD_DIM
    rows = t_len // GRID_W
    kr = min(WIN_R, rows)
    assert kr % 2 == 0, "the bias table pairs key rows"
    tc = kc.shape[2]
    bias = _na_bias_table(rpb_l)
    rps = _tile(rows, NA_ROWS_PER_STEP)
    return pl.pallas_call(
        functools.partial(_na_latent_kernel, rows=rows, kr=kr, rps=rps, scale=HEAD_DIM ** -0.5),
        grid=(bsz, rows // rps),
        in_specs=[pl.BlockSpec((1, rps * GRID_W, w), lambda b, r: (b, r, q_col)),
                  pl.BlockSpec((1, t_len, w), lambda b, r: (b, 0, k_col)),
                  pl.BlockSpec((1, t_len, w), lambda b, r: (b, 0, v_col)),
                  pl.BlockSpec((1, 1, tc, w), lambda b, r: (b, layer, 0, 0)),
                  pl.BlockSpec((1, 1, tc, w), lambda b, r: (b, layer, 0, 0)),
                  _const_spec(bias)],
        out_specs=pl.BlockSpec((1, rps * GRID_W, w), lambda b, r: (b, r, 0)),
        out_shape=jax.ShapeDtypeStruct((bsz, t_len, w), BF16),
        compiler_params=_cparams("parallel", "parallel"),
    )(q, k, v, kc, vc, bias)


def _mla_k_part(ckv_bf16, krope, wukv_ref, gk_ref, ek_ref, tile_ref, rope_tabs):
    nope_w = MLA_HEADS * NOPE_DIM
    kv = _dot(ckv_bf16, wukv_ref[...])
    k_nope = kv[:, :nope_w]
    kr_t = _sel_dot(krope, tile_ref[...])
    sq = jnp.concatenate([k_nope * k_nope, krope * krope], axis=-1)
    ms = _sel_dot(sq, ek_ref[...]) * (1.0 / MLA_QK)
    rs = lax.rsqrt(ms + EPS)
    g = gk_ref[...]
    kn = k_nope * rs[:, :nope_w] * g[:, :nope_w]
    kro = kr_t * rs[:, nope_w:] * g[:, nope_w:]
    if rope_tabs is not None:
        kro = _rope_rotate(kro, rope_tabs[0][...], rope_tabs[1][...])
    return jnp.concatenate([kn, kro], axis=-1), kv[:, nope_w:]


def _mla_prep_kernel(*refs, rope):
    it = iter(refs)
    x_ref = next(it)
    gcq_ref, gckv_ref, wuq_ref, wukv_ref, gq_ref, gk_ref, eq_ref, ek_ref, tile_ref = (next(it) for _ in range(9))
    tabs = (next(it), next(it)) if rope else None
    q_ref, k_ref, v_ref, ckv_ref, kr_ref = (next(it) for _ in range(5))
    nope_w = MLA_HEADS * NOPE_DIM

    x = x_ref[0]
    dq = x[:, :Q_RANK]
    cq = dq * lax.rsqrt(jnp.mean(dq * dq, axis=-1, keepdims=True) + EPS) * gcq_ref[...]
    dkv = x[:, Q_RANK:Q_RANK + KV_RANK]
    ckv = dkv * lax.rsqrt(jnp.mean(dkv * dkv, axis=-1, keepdims=True) + EPS) * gckv_ref[...]
    krope = x[:, Q_RANK + KV_RANK:]
    ckv_ref[0] = ckv
    kr_ref[0] = krope

    qr = _dot(cq.astype(BF16), wuq_ref[...])
    ms = _sel_dot(qr * qr, eq_ref[...]) * (1.0 / MLA_QK)
    qn = qr * lax.rsqrt(ms + EPS) * gq_ref[...]
    if rope:
        q_ref[0] = jnp.concatenate(
            [qn[:, :nope_w], _rope_rotate(qn[:, nope_w:], tabs[0][...], tabs[1][...])], axis=-1)
    else:
        q_ref[0] = qn
    kn, v = _mla_k_part(ckv.astype(BF16), krope, wukv_ref, gk_ref, ek_ref, tile_ref, tabs)
    k_ref[0] = kn
    v_ref[0] = v.astype(v_ref.dtype)


def _mla_ctx_kernel(ckv_ref, kr_ref, wukv_ref, gk_ref, ek_ref, tile_ref, k_ref, v_ref):
    kn, v = _mla_k_part(ckv_ref[0, 0].astype(BF16), kr_ref[0, 0], wukv_ref, gk_ref, ek_ref, tile_ref, None)
    k_ref[0, 0] = kn
    v_ref[0, 0] = v.astype(v_ref.dtype)


def _mla_consts():
    nope_w = MLA_HEADS * NOPE_DIM
    qk_w = nope_w + MLA_HEADS * ROPE_DIM
    head_q = np.concatenate([np.arange(nope_w) // NOPE_DIM, np.arange(MLA_HEADS * ROPE_DIM) // ROPE_DIM])
    eq = head_q[:, None] == head_q[None, :]
    ek = np.concatenate([eq[:nope_w], np.ones((ROPE_DIM, qk_w), bool)], axis=0)
    tile = np.arange(ROPE_DIM)[:, None] == (np.arange(MLA_HEADS * ROPE_DIM) % ROPE_DIM)[None, :]
    return jnp.asarray(eq, BF16), jnp.asarray(ek, BF16), jnp.asarray(tile, BF16)


def _mla_perm_weights(w_uq_l, w_ukv_l, g_q, g_k):
    wq = w_uq_l.reshape(Q_RANK, MLA_HEADS, MLA_QK)
    wq = jnp.concatenate([wq[:, :, :NOPE_DIM].reshape(Q_RANK, -1), wq[:, :, NOPE_DIM:].reshape(Q_RANK, -1)], axis=1)
    wkv = w_ukv_l.reshape(KV_RANK, MLA_HEADS, NOPE_DIM + V_DIM)
    wkv = jnp.concatenate([wkv[:, :, :NOPE_DIM].reshape(KV_RANK, -1), wkv[:, :, NOPE_DIM:].reshape(KV_RANK, -1)], axis=1)

    def gain(g):
        return jnp.concatenate([jnp.tile(g[:NOPE_DIM], MLA_HEADS), jnp.tile(g[NOPE_DIM:], MLA_HEADS)]).reshape(1, -1)

    return wq.astype(BF16), wkv.astype(BF16), gain(g_q), gain(g_k)


def _const_spec(a):
    nd = a.ndim
    return pl.BlockSpec(a.shape, lambda *_: (0,) * nd)


def _mla_prep(x, g_cq, g_ckv, wq, wkv, gq, gk, consts, rope):
    bsz, t_len, w = x.shape
    tt = _tile(t_len, 256)
    eq, ek, tile = consts
    qk_w = MLA_HEADS * MLA_QK
    args = [x, g_cq.reshape(1, -1), g_ckv.reshape(1, -1), wq, wkv, gq, gk, eq, ek, tile]
    in_specs = [pl.BlockSpec((1, tt, w), lambda b, t: (b, t, 0))] + [_const_spec(a) for a in args[1:]]
    if rope is not None:
        rw = MLA_HEADS * ROPE_DIM
        in_specs += [pl.BlockSpec((tt, rw), lambda b, t: (t, 0))] * 2
        args += [rope[0][:, :rw], rope[1][:, :rw]]
    widths = (qk_w, qk_w, MLA_HEADS * V_DIM, KV_RANK, ROPE_DIM)
    dtypes = (F32, F32, BF16, F32, F32)
    return pl.pallas_call(
        functools.partial(_mla_prep_kernel, rope=rope is not None),
        grid=(bsz, t_len // tt), in_specs=in_specs,
        out_specs=[pl.BlockSpec((1, tt, wd), lambda b, t: (b, t, 0)) for wd in widths],
        out_shape=[jax.ShapeDtypeStruct((bsz, t_len, wd), dt) for wd, dt in zip(widths, dtypes)],
        compiler_params=_cparams("parallel", "parallel"),
    )(*args)


def _mla_ctx(cache_ckv, cache_krope, layer, wkv, gk, consts):
    bsz, depth, tc, _ = cache_ckv.shape
    _, ek, tile = consts
    qk_w = MLA_HEADS * MLA_QK
    vw = MLA_HEADS * V_DIM
    args = [cache_ckv, cache_krope, wkv, gk, ek, tile]
    in_specs = [pl.BlockSpec((1, 1, tc, KV_RANK), lambda b: (b, layer, 0, 0)),
                pl.BlockSpec((1, 1, tc, ROPE_DIM), lambda b: (b, layer, 0, 0))]
    in_specs += [_const_spec(a) for a in args[2:]]
    return pl.pallas_call(
        _mla_ctx_kernel, grid=(bsz,), in_specs=in_specs,
        out_specs=[pl.BlockSpec((1, 1, tc, qk_w), lambda b: (b, 0, 0, 0)),
                   pl.BlockSpec((1, 1, tc, vw), lambda b: (b, 0, 0, 0))],
        out_shape=[jax.ShapeDtypeStruct((bsz, 1, tc, qk_w), F32), jax.ShapeDtypeStruct((bsz, 1, tc, vw), BF16)],
        compiler_params=_cparams("parallel"),
    )(*args)


def _ssd_prep_kernel(x_ref, cw_ref, cb_ref, dtb_ref, xs_ref, bm_ref, cm_ref, dt_ref):
    t_len = x_ref.shape[1]
    cw_total = cw_ref.shape[1]
    xbc = x_ref[0, :, BRANCH_W:BRANCH_W + cw_total]
    row = lax.broadcasted_iota(jnp.int32, xbc.shape, 0)
    prev1 = jnp.where(row >= 1, pltpu.roll(xbc, 1, 0), 0.0)
    next1 = jnp.where(row < t_len - 1, pltpu.roll(xbc, t_len - 1, 0), 0.0)
    next2 = jnp.where(row < t_len - 2, pltpu.roll(xbc, t_len - 2, 0), 0.0)
    cw = cw_ref[...]
    u = prev1 * cw[0:1] + xbc * cw[1:2] + next1 * cw[2:3] + next2 * cw[3:4] + cb_ref[...]
    u = u * jax.nn.sigmoid(u)
    xw = SSD_HEADS * SSD_HEAD_DIM
    bw = SSD_GROUPS * D_STATE
    xs_ref[0] = u[:, :xw]
    bm_ref[0] = u[:, xw:xw + bw]
    cm_ref[0] = u[:, xw + bw:]
    raw = x_ref[0, :, BRANCH_W + cw_total:] + dtb_ref[...]
    dt_ref[0] = jnp.maximum(raw, 0.0) + jnp.log1p(jnp.exp(-jnp.abs(raw)))


def _ssd_prep(g3, conv_w, conv_b, dt_bias):
    bsz, t_len, w = g3.shape
    cw = conv_w.shape[1]
    xw = SSD_HEADS * SSD_HEAD_DIM
    bw = SSD_GROUPS * D_STATE
    nh2 = 2 * SSD_HEADS
    widths = (xw, bw, bw, nh2)
    return pl.pallas_call(
        _ssd_prep_kernel, grid=(bsz,),
        in_specs=[pl.BlockSpec((1, t_len, w), lambda b: (b, 0, 0)),
                  pl.BlockSpec(conv_w.shape, lambda b: (0, 0)),
                  pl.BlockSpec((1, cw), lambda b: (0, 0)),
                  pl.BlockSpec((1, nh2), lambda b: (0, 0))],
        out_specs=[pl.BlockSpec((1, t_len, wd), lambda b: (b, 0, 0)) for wd in widths],
        out_shape=[jax.ShapeDtypeStruct((bsz, t_len, wd), F32) for wd in widths],
        compiler_params=_cparams("parallel"),
    )(g3, conv_w, conv_b.reshape(1, cw), dt_bias.reshape(1, nh2))


def _ssd_scan_kernel(*refs, has_s0, nc):
    if has_s0:
        xs_ref, bm_ref, cm_ref, dt_ref, dtt_ref, alogt_ref, s0_ref, y_ref, sfin_ref, st_ref = refs
    else:
        xs_ref, bm_ref, cm_ref, dt_ref, dtt_ref, alogt_ref, y_ref, sfin_ref, st_ref = refs
    d = pl.program_id(1)
    c = pl.program_id(2)
    chunk = xs_ref.shape[1]

    @pl.when(c == 0)
    def _():
        if has_s0:
            st_ref[...] = s0_ref[0, 0, 0]
        else:
            st_ref[...] = jnp.zeros_like(st_ref)

    a_col = -jnp.exp(alogt_ref[0])
    dt = dt_ref[0, 0]
    dta_t = dtt_ref[0, 0] * a_col
    ri = lax.broadcasted_iota(jnp.int32, (chunk, chunk), 0)
    ci = lax.broadcasted_iota(jnp.int32, (chunk, chunk), 1)
    sign = 1 - 2 * d
    mask = (ri - ci) * sign >= 0
    mask_t = jnp.where((ci - ri) * sign >= 0, 1.0, 0.0).astype(BF16)
    acs_rows = _sel_dot(dta_t, mask_t)

    xs = xs_ref[0]
    rep = SSD_HEADS // SSD_GROUPS
    cb = []
    for g in range(SSD_GROUPS):
        cg = cm_ref[0, :, g * D_STATE:(g + 1) * D_STATE].astype(BF16)
        bg = bm_ref[0, :, g * D_STATE:(g + 1) * D_STATE].astype(BF16)
        cb.append((cg, bg, _dot_nt(cg, bg)))

    heads = range(SSD_HEADS)
    dta_rows = [dta_t[h:h + 1, :] for h in heads]
    acs_cols = [jnp.sum(jnp.where(mask, dta_rows[h], 0.0), axis=-1, keepdims=True) for h in heads]
    tots = [jnp.sum(dta_rows[h], axis=-1, keepdims=True) for h in heads]
    decays = [jnp.exp(jnp.where(mask, acs_cols[h] - acs_rows[h:h + 1, :], -jnp.inf)) for h in heads]
    xdts = [xs[:, h * SSD_HEAD_DIM:(h + 1) * SSD_HEAD_DIM] * dt[:, h:h + 1] for h in heads]
    states = [st_ref[h] for h in heads]
    y_diag = [_dot((cb[h // rep][2] * decays[h]).astype(BF16), xdts[h].astype(BF16)) for h in heads]
    y_off = [_dot_nt(cb[h // rep][0], states[h].astype(BF16)) for h in heads]
    upd = [_dot_tn((xdts[h] * jnp.exp(tots[h] - acs_cols[h])).astype(BF16), cb[h // rep][1]) for h in heads]
    for h in heads:
        st_ref[h] = states[h] * jnp.exp(tots[h]) + upd[h]
        y_ref[0, 0, :, h * SSD_HEAD_DIM:(h + 1) * SSD_HEAD_DIM] = y_diag[h] + jnp.exp(acs_cols[h]) * y_off[h]

    @pl.when(c == nc - 1)
    def _():
        sfin_ref[0, 0] = st_ref[...]


def _ssd_scan(xs, bm, cm, dt, a_log_l, s0=None, layer=0):
    bsz, t_len, xw = xs.shape
    chunk = CHUNK if t_len % CHUNK == 0 else t_len
    nc = t_len // chunk
    nh = SSD_HEADS
    bw = bm.shape[-1]
    dt4 = dt.reshape(bsz, t_len, 2, nh).transpose(0, 2, 1, 3)
    dtt = dt4.transpose(0, 1, 3, 2)

    def tok(b, d, c):
        return c + d * (nc - 1 - 2 * c)

    in_specs = [pl.BlockSpec((1, chunk, xw), lambda b, d, c: (b, tok(b, d, c), 0)),
                pl.BlockSpec((1, chunk, bw), lambda b, d, c: (b, tok(b, d, c), 0)),
                pl.BlockSpec((1, chunk, bw), lambda b, d, c: (b, tok(b, d, c), 0)),
                pl.BlockSpec((1, 1, chunk, nh), lambda b, d, c: (b, d, tok(b, d, c), 0)),
                pl.BlockSpec((1, 1, nh, chunk), lambda b, d, c: (b, d, 0, tok(b, d, c))),
                pl.BlockSpec((1, nh, 1), lambda b, d, c: (d, 0, 0))]
    args = [xs, bm, cm, dt4, dtt, a_log_l.reshape(2, nh, 1)]
    if s0 is not None:
        in_specs.append(pl.BlockSpec((1, 1, 1, nh, SSD_HEAD_DIM, D_STATE), lambda b, d, c: (b, layer, d, 0, 0, 0)))
        args.append(s0)
    return pl.pallas_call(
        functools.partial(_ssd_scan_kernel, has_s0=s0 is not None, nc=nc),
        grid=(bsz, 2, nc), in_specs=in_specs,
        out_specs=[pl.BlockSpec((1, 1, chunk, xw), lambda b, d, c: (d, b, tok(b, d, c), 0)),
                   pl.BlockSpec((1, 1, nh, SSD_HEAD_DIM, D_STATE), lambda b, d, c: (b, d, 0, 0, 0))],
        out_shape=[jax.ShapeDtypeStruct((2, bsz, t_len, xw), F32),
                   jax.ShapeDtypeStruct((bsz, 2, nh, SSD_HEAD_DIM, D_STATE), F32)],
        scratch_shapes=[pltpu.VMEM((nh, SSD_HEAD_DIM, D_STATE), F32)],
        compiler_params=_cparams("parallel", "arbitrary", "arbitrary"),
    )(*args)


def _ssd_post_kernel(yf_ref, yb_ref, xs_ref, z_ref, dsk_ref, g_ref, o_ref):
    z = z_ref[0]
    y = yf_ref[0, 0] + yb_ref[0, 0] + dsk_ref[...] * xs_ref[0]
    y = y * (z * jax.nn.sigmoid(z))
    y = y * lax.rsqrt(jnp.mean(y * y, axis=-1, keepdims=True) + EPS) * g_ref[...]
    o_ref[0] = y.astype(o_ref.dtype)


def _ssd_post(y2, xs, g3, d_skip, g_ssd):
    bsz, t_len, xw = xs.shape
    tt = _tile(t_len, 256)
    dsk = jnp.repeat(d_skip, SSD_HEAD_DIM).reshape(1, xw)
    return pl.pallas_call(
        _ssd_post_kernel, grid=(bsz, t_len // tt),
        in_specs=[pl.BlockSpec((1, 1, tt, xw), lambda b, t: (0, b, t, 0)),
                  pl.BlockSpec((1, 1, tt, xw), lambda b, t: (1, b, t, 0)),
                  pl.BlockSpec((1, tt, xw), lambda b, t: (b, t, 0)),
                  pl.BlockSpec((1, tt, xw), lambda b, t: (b, t, 0)),
                  pl.BlockSpec((1, xw), lambda b, t: (0, 0)),
                  pl.BlockSpec((1, xw), lambda b, t: (0, 0))],
        out_specs=pl.BlockSpec((1, tt, xw), lambda b, t: (b, t, 0)),
        out_shape=jax.ShapeDtypeStruct((bsz, t_len, xw), BF16),
        compiler_params=_cparams("parallel", "parallel"),
    )(y2, y2, xs, g3, dsk, g_ssd.reshape(1, xw))


def _merge_kernel(h_ref, wg0_ref, wg1_ref, wg2_ref, wg3_ref, ya_ref, yb_ref, yc_ref, yd_ref, wb_ref, o_ref):
    h = h_ref[0]
    acc = None
    wg_refs = (wg0_ref, wg1_ref, wg2_ref, wg3_ref)
    for i, y_ref in enumerate((ya_ref, yb_ref, yc_ref, yd_ref)):
        gate = jax.nn.sigmoid(_dot(h, wg_refs[i][...]))
        term = gate * _dot(y_ref[0], wb_ref[i].astype(BF16))
        acc = term if acc is None else acc + term
    o_ref[0] = acc.astype(o_ref.dtype)


def _merge(h, wg, ys, w_branch, layer):
    bsz, t_len, d = h.shape
    tm = _tile(t_len, 512)
    tn = _tile(d, 512)
    y_spec = pl.BlockSpec((1, tm, BRANCH_W), lambda n, b, t: (b, t, 0))
    npb = d // tn

    def wg_spec(i):
        return pl.BlockSpec((d, tn), lambda n, b, t: (0, i * npb + n))

    return pl.pallas_call(
        _merge_kernel, grid=(npb, bsz, t_len // tm),
        in_specs=[pl.BlockSpec((1, tm, d), lambda n, b, t: (b, t, 0)),
                  wg_spec(0), wg_spec(1), wg_spec(2), wg_spec(3),
                  y_spec, y_spec, y_spec, y_spec,
                  pl.BlockSpec((None, N_BRANCH, BRANCH_W, tn), lambda n, b, t: (layer, 0, 0, n))],
        out_specs=pl.BlockSpec((1, tm, tn), lambda n, b, t: (b, t, n)),
        out_shape=jax.ShapeDtypeStruct((bsz, t_len, d), BF16),
        compiler_params=_cparams("parallel", "parallel", "parallel"),
    )(h, wg, wg, wg, wg, *ys, w_branch)


def _moe_kernel(blk_e_ref, nused_ref, tok_ref, dst_ref, x_hbm, wg_ref, wu_ref, bg_ref, bu_ref,
                wd_ref, bd_ref, y_hbm, xbuf, xb, acc, gsem, ssem, *, ts, nf, nblk, dump0):
    i = pl.program_id(0)
    f = pl.program_id(1)
    nu = nused_ref[0]
    slot = i % 2
    rps = ts // nf

    def start_gather(base, s, r):
        pltpu.make_async_copy(x_hbm.at[pl.ds(tok_ref[base + r], 1)], xbuf.at[s, pl.ds(r, 1)], gsem.at[s]).start()

    def start_scatter(base, s, r):
        dst = jnp.where(i == 0, dump0 + r, dst_ref[base + r])
        pltpu.make_async_copy(acc.at[s, pl.ds(r, 1)], y_hbm.at[pl.ds(dst, 1)], ssem.at[s]).start()

    def gather_wait(s):
        pltpu.make_async_copy(x_hbm.at[pl.ds(0, ts)], xbuf.at[s], gsem.at[s]).wait()

    def scatter_wait(s):
        pltpu.make_async_copy(acc.at[s], y_hbm.at[pl.ds(0, ts)], ssem.at[s]).wait()

    prev_base = jnp.maximum(i - 1, 0) * ts
    next_base = jnp.minimum(i + 1, nblk - 1) * ts
    row0 = f * rps

    @pl.when(f == 0)
    def _():
        @pl.when((i >= 1) & (i <= nu + 1))
        def _():
            scatter_wait(slot)

        @pl.when(i == 0)
        def _():
            def body(r, carry):
                start_gather(0, 0, r)
                return carry

            lax.fori_loop(0, ts, body, 0, unroll=8)
            acc[1] = jnp.zeros(acc.shape[1:], acc.dtype)

        @pl.when(i <= nu)
        def _():
            gather_wait(slot)

        @pl.when(i < nu)
        def _():
            xb[...] = xbuf[slot].astype(BF16)
            for r in range(ts):
                start_gather(next_base, 1 - slot, r)
                start_scatter(prev_base, 1 - slot, r)

    @pl.when(i < nu)
    def _():
        x = xb[...]
        g = _dot(x, wg_ref[...].astype(BF16)) + bg_ref[...]
        u = _dot(x, wu_ref[...].astype(BF16)) + bu_ref[...]
        g = jnp.minimum(g, SWIGLU_LIMIT)
        u = jnp.clip(u, -SWIGLU_LIMIT, SWIGLU_LIMIT)
        act = g * jax.nn.sigmoid(SWIGLU_ALPHA * g) * (u + 1.0)
        part = _dot(act.astype(BF16), wd_ref[...].astype(BF16))

        @pl.when(f == 0)
        def _():
            acc[slot] = part + bd_ref[...]

        @pl.when(f > 0)
        def _():
            acc[slot] += part

    @pl.when(i == nu)
    def _():
        def body(k, carry):
            start_scatter(prev_base, 1 - slot, row0 + k)
            return carry

        lax.fori_loop(0, rps, body, 0, unroll=8)


def _moe_blocks(h2, blk_e, nused, slot_tok, slot_dst, n_rows_out, w_gu, b_gu, w_dn, b_dn, layer):
    d = h2.shape[1]
    n_exp, _, ff2 = w_gu.shape[1:]
    ff = ff2 // 2
    ts, fc = MOE_TOKENS, _tile(ff, MOE_FF_CHUNK)
    nf = ff // fc
    nblk = slot_tok.shape[0] // ts

    def fcl(i, f, nu):
        return jnp.where(i < nu[0], f, nf - 1)

    grid_spec = pltpu.PrefetchScalarGridSpec(
        num_scalar_prefetch=4, grid=(nblk + 2, nf),
        in_specs=[
            pl.BlockSpec(memory_space=pl.ANY),
            pl.BlockSpec((None, None, d, fc), lambda i, f, be, nu, *_: (layer, be[i], 0, fcl(i, f, nu))),
            pl.BlockSpec((None, None, d, fc), lambda i, f, be, nu, *_: (layer, be[i], 0, nf + fcl(i, f, nu))),
            pl.BlockSpec((None, None, 1, fc), lambda i, f, be, nu, *_: (layer, be[i], 0, fcl(i, f, nu))),
            pl.BlockSpec((None, None, 1, fc), lambda i, f, be, nu, *_: (layer, be[i], 0, nf + fcl(i, f, nu))),
            pl.BlockSpec((None, None, fc, d), lambda i, f, be, nu, *_: (layer, be[i], fcl(i, f, nu), 0)),
            pl.BlockSpec((None, None, 1, d), lambda i, f, be, nu, *_: (layer, be[i], 0, 0)),
        ],
        out_specs=pl.BlockSpec(memory_space=pl.ANY),
        scratch_shapes=[pltpu.VMEM((2, ts, d), F32), pltpu.VMEM((ts, d), BF16), pltpu.VMEM((2, ts, d), F32),
                        pltpu.SemaphoreType.DMA((2,)), pltpu.SemaphoreType.DMA((2,))],
    )
    depth = w_gu.shape[0]
    return pl.pallas_call(
        functools.partial(_moe_kernel, ts=ts, nf=nf, nblk=nblk, dump0=n_rows_out - ts), grid_spec=grid_spec,
        out_shape=jax.ShapeDtypeStruct((n_rows_out, d), F32),
        compiler_params=_cparams("arbitrary", "arbitrary"),
    )(blk_e, nused, slot_tok, slot_dst, h2, w_gu, w_gu, b_gu.reshape(depth, n_exp, 1, ff2),
      b_gu.reshape(depth, n_exp, 1, ff2), w_dn, b_dn.reshape(depth, n_exp, 1, d))


def _moe(h2, logits, w_gu, b_gu, w_dn, b_dn, layer):
    n_tok, d = h2.shape
    n_exp = logits.shape[1]
    ts = MOE_TOKENS
    top_v, top_i = lax.top_k(logits, TOP_K)
    gate_w = jax.nn.softmax(top_v, axis=-1)
    n_asg = n_tok * TOP_K
    flat_e = top_i.reshape(-1)
    onehot = (flat_e[:, None] == jnp.arange(n_exp, dtype=flat_e.dtype)[None, :]).astype(jnp.int32)
    counts = jnp.sum(onehot, axis=0)
    padded = (counts + ts - 1) // ts * ts
    pad_ends = jnp.cumsum(padded)
    pad_starts = pad_ends - padded
    pos = jnp.sum(onehot * (jnp.cumsum(onehot, axis=0) - 1 + pad_starts[None, :]), axis=1).astype(jnp.int32)
    nblk = -(-n_asg // ts) + n_exp
    cap = nblk * ts
    slot_asg = jnp.full((cap,), -1, jnp.int32).at[pos].set(jnp.arange(n_asg, dtype=jnp.int32))
    valid = slot_asg >= 0
    slot_tok = jnp.where(valid, slot_asg // TOP_K, 0)
    slot_dst = jnp.where(valid, (slot_asg % TOP_K) * n_tok + slot_asg // TOP_K,
                         n_asg + jnp.arange(cap, dtype=jnp.int32) % ts)
    nused = (pad_ends[-1] // ts).astype(jnp.int32)
    blk = jnp.arange(nblk + 2, dtype=jnp.int32)
    blk_e = jnp.sum((pad_ends[None, :] <= (blk * ts)[:, None]).astype(jnp.int32), axis=1)
    blk_e = jnp.minimum(blk_e, n_exp - 1)
    last_e = jnp.sum(jnp.where(blk == nused - 1, blk_e, 0))
    blk_e = jnp.where(blk < nused, blk_e, last_e).astype(jnp.int32)
    y4 = _moe_blocks(h2, blk_e, nused.reshape(1), slot_tok, slot_dst, n_asg + ts, w_gu, b_gu, w_dn, b_dn, layer)
    return y4, gate_w


def _combine_kernel(x_ref, g_ref, w_ref, y0_ref, y1_ref, y2_ref, y3_ref, o_ref):
    w = w_ref[...]
    m = None
    for j, y_ref in enumerate((y0_ref, y1_ref, y2_ref, y3_ref)):
        term = w[:, j:j + 1] * y_ref[...]
        m = term if m is None else m + term
    o_ref[0] = x_ref[0] + g_ref[0] * m


def _moe_combine(x, gate, y4, gate_w, row0):
    bsz, t_len, d = x.shape
    n_tok = gate_w.shape[0]
    tt = _tile(t_len, 256)
    tpb = t_len // tt
    t0 = row0 // tt
    nt = n_tok // tt

    def y_spec(j):
        return pl.BlockSpec((tt, d), lambda b, t: (j * nt + t0 + b * tpb + t, 0))

    if gate.shape[0] == 1:
        g_spec = pl.BlockSpec((1, 1, d), lambda b, t: (0, 0, 0))
    else:
        g_spec = pl.BlockSpec((1, 1, d), lambda b, t: (b, 0, 0))
    return pl.pallas_call(
        _combine_kernel, grid=(bsz, tpb),
        in_specs=[pl.BlockSpec((1, tt, d), lambda b, t: (b, t, 0)), g_spec,
                  pl.BlockSpec((tt, TOP_K), lambda b, t: (t0 + b * tpb + t, 0)),
                  y_spec(0), y_spec(1), y_spec(2), y_spec(3)],
        out_specs=pl.BlockSpec((1, tt, d), lambda b, t: (b, t, 0)),
        out_shape=jax.ShapeDtypeStruct((bsz, t_len, d), F32),
        compiler_params=_cparams("parallel", "parallel"),
    )(x, gate, gate_w, y4, y4, y4, y4)


def _token_mixers(h, seq, lw, layer, lam_init, ctx):
    latent = ctx is not None
    bsz, t_len = seq
    flat = h.shape[:2]
    rope = _rope_tables(t_len, BRANCH_W) if latent else None
    qkv = _qkv_proj(h, lw['w_qkv'], lw['qkv_gains'], rope).reshape(bsz, t_len, -1)
    g2 = _mm(h, lw['w_lat']).reshape(bsz, t_len, -1)
    g3 = _mm(h, lw['w_ssd']).reshape(bsz, t_len, -1)

    if latent:
        ya = _na_latent(qkv, qkv, qkv, 2, ctx['na_k'], ctx['na_v'], layer, lw['rpb'], q_col=0, k_col=1)
    else:
        ya = _attention(qkv, qkv, qkv, 2, heads=NA_HEADS, maps=1, pieces=_na_pieces, dv=HEAD_DIM,
                        scale=HEAD_DIM ** -0.5, q_col=0, k_col=1, qk_width=BRANCH_W)

    diff = (lw['lam_q1'], lw['lam_k1'], lw['lam_q2'], lw['lam_k2'], lw['g_df_sub'], lam_init)
    df_ctx = (ctx['df_k'], ctx['df_v'], layer, 0) if latent else None
    yb = _attention(qkv, qkv, qkv, 5, heads=DF_HEADS, maps=2, pieces=_df_pieces, dv=2 * HEAD_DIM,
                    scale=HEAD_DIM ** -0.5, ctx=df_ctx, diff=diff, q_col=3, k_col=4, qk_width=BRANCH_W)

    consts = _mla_consts()
    qc, kc, vc, ckv, krope = _mla_prep(g2, lw['g_mla_cq'], lw['g_mla_ckv'], lw['w_uq'], lw['w_ukv'],
                                       lw['g_mla_q'], lw['g_mla_k'], consts, rope)
    mla_ctx = None
    if latent:
        kcx, vcx = _mla_ctx(ctx['mla_ckv'], ctx['mla_krope'], layer, lw['w_ukv'], lw['g_mla_k'], consts)
        mla_ctx = (kcx, vcx, 0, 0)
    yc = _attention(qc, kc, vc, 0, heads=MLA_HEADS, maps=1, pieces=_mla_pieces, dv=V_DIM,
                    scale=MLA_QK ** -0.5, ctx=mla_ctx)

    xs, bm, cm, dt = _ssd_prep(g3, lw['conv_w'], lw['conv_b'], lw['dt_bias'])
    y2, s_fin = _ssd_scan(xs, bm, cm, dt, lw['a_log'], ctx['ssd'] if latent else None, layer)
    yd = _ssd_post(y2, xs, g3, lw['d_skip'], lw['g_ssd'])

    ys = tuple(y.reshape(flat + (BRANCH_W,)) for y in (ya, yb, yc, yd))
    merged = _merge(h, lw['w_gate'], ys, lw['w_branch'], layer)
    if latent:
        return merged, None

    def group(n, *head_shape):
        return qkv[..., n * BRANCH_W:(n + 1) * BRANCH_W].reshape((bsz, t_len) + head_shape)

    new = dict(
        na_k=group(1, NA_HEADS, HEAD_DIM), na_v=group(2, NA_HEADS, HEAD_DIM),
        df_k=group(4, DF_HEADS, 2, HEAD_DIM), df_v=group(5, DF_HEADS, 2 * HEAD_DIM),
        mla_ckv=ckv, mla_krope=krope, ssd=s_fin)
    return merged, new


def kernel(x_prompt, x_sample, cache_na_k, cache_na_v, cache_df_k, cache_df_v, cache_mla_ckv, cache_mla_krope, state_ssd, c, c_ctx, w_mod, b_mod, g_norm1, g_norm2, w_in, g_na_q, g_na_k, rpb, g_df_q, g_df_k, lam_q1, lam_k1, lam_q2, lam_k2, g_df_sub, g_mla_cq, g_mla_ckv, w_uq, w_ukv, g_mla_q, g_mla_k, conv_w, conv_b, dt_bias, a_log, d_skip, g_ssd, w_branch, w_o, w_router, b_router, w_gu, b_gu, w_dn, b_dn):
    depth, d, _ = w_in.shape
    dec_b, _, past = cache_na_k.shape[:3]
    n_p = x_prompt.shape[0] * x_prompt.shape[1]
    n_exp = w_router.shape[-1]
    xp, xs = x_prompt.reshape(1, n_p, d), x_sample

    n_rows = -(-(dec_b + 1) // 8) * 8
    cc = jnp.zeros((n_rows, d), F32).at[:dec_b].set(c).at[dec_b].set(c_ctx)
    mods = _mod_params(cc, w_mod, b_mod)

    ctx_all = dict(
        na_k=cache_na_k.reshape(dec_b, depth, past, NA_HEADS * HEAD_DIM),
        na_v=cache_na_v.reshape(dec_b, depth, past, NA_HEADS * HEAD_DIM),
        df_k=cache_df_k.reshape(dec_b, depth, past, DF_HEADS * 2 * HEAD_DIM),
        df_v=cache_df_v.reshape(dec_b, depth, past, DF_HEADS * 2 * HEAD_DIM),
        mla_ckv=cache_mla_ckv, mla_krope=cache_mla_krope, ssd=state_ssd)

    o_lat = 6 * BRANCH_W
    o_ssd = o_lat + Q_RANK + KV_RANK + ROPE_DIM
    o_gate = o_ssd + BRANCH_W + conv_w.shape[-1] + 2 * SSD_HEADS
    ne_pad = -(-n_exp // LANE) * LANE

    names = ('na_k', 'na_v', 'df_k', 'df_v', 'mla_ckv', 'mla_krope', 'ssd')
    new = {n: [] for n in names}
    for l in range(depth):
        lam_init = 0.8 - 0.6 * math.exp(-0.3 * l)
        wq, wkv, gq, gk = _mla_perm_weights(w_uq[l], w_ukv[l], g_mla_q[l], g_mla_k[l])
        w_l = w_in[l]
        lw = dict(
            w_qkv=w_l[:, :o_lat].astype(BF16), w_lat=w_l[:, o_lat:o_ssd].astype(BF16),
            w_ssd=w_l[:, o_ssd:o_gate].astype(BF16),
            w_gate=w_l[:, o_gate:].astype(BF16),
            qkv_gains=jnp.stack([jnp.tile(g, BRANCH_W // HEAD_DIM) for g in (
                g_na_q[l], g_na_k[l], jnp.ones_like(g_na_q[l]), g_df_q[l], g_df_k[l], jnp.ones_like(g_df_q[l])
            )]).reshape(QKV_GROUPS, 1, BRANCH_W),
            rpb=rpb[l], lam_q1=lam_q1[l], lam_k1=lam_k1[l], lam_q2=lam_q2[l], lam_k2=lam_k2[l], g_df_sub=g_df_sub[l],
            g_mla_cq=g_mla_cq[l], g_mla_ckv=g_mla_ckv[l], w_uq=wq, w_ukv=wkv, g_mla_q=gq, g_mla_k=gk,
            conv_w=conv_w[l], conv_b=conv_b[l], dt_bias=dt_bias[l], a_log=a_log[l], d_skip=d_skip[l],
            g_ssd=g_ssd[l], w_branch=w_branch)
        mc = [m[:, None, :] for m in jnp.split(mods[l, :dec_b], 6, axis=-1)]
        mx = [m[:, None, :] for m in jnp.split(mods[l, dec_b:dec_b + 1], 6, axis=-1)]

        hp = _norm_mod(xp, g_norm1[l], mx[0], mx[1])
        mp, ctx_new = _token_mixers(hp, x_prompt.shape[:2], lw, l, lam_init, None)
        xp = _mm_residual(mp, w_o, l, xp, mx[2])
        hs = _norm_mod(xs, g_norm1[l], mc[0], mc[1])
        ms, _ = _token_mixers(hs, x_sample.shape[:2], lw, l, lam_init, ctx_all)
        xs = _mm_residual(ms, w_o, l, xs, mc[2])

        wr = jnp.zeros((d, ne_pad), F32).at[:, :n_exp].set(w_router[l])
        br = jnp.zeros((1, ne_pad), F32).at[0, :n_exp].set(b_router[l])
        hp2, lg_p = _norm_mod(xp, g_norm2[l], mx[3], mx[4], router=(wr, br))
        hs2, lg_s = _norm_mod(xs, g_norm2[l], mc[3], mc[4], router=(wr, br))
        h2 = jnp.concatenate([hp2.reshape(-1, d), hs2.reshape(-1, d)], axis=0)
        logits = jnp.concatenate([lg_p.reshape(-1, ne_pad), lg_s.reshape(-1, ne_pad)], axis=0)[:, :n_exp]
        y4, gate_w = _moe(h2, logits, w_gu, b_gu, w_dn, b_dn, l)
        xp = _moe_combine(xp, mx[5], y4, gate_w, 0)
        xs = _moe_combine(xs, mc[5], y4, gate_w, n_p)
        for n in names:
            new[n].append(ctx_new[n])
    return (xp.reshape(x_prompt.shape), xs) + tuple(jnp.stack(new[n], axis=1) for n in names)
```

```python
import functools
import math

import jax
import jax.numpy as jnp
from jax import lax
import numpy as np
from jax.experimental import pallas as pl
from jax.experimental.pallas import tpu as pltpu

F32 = jnp.float32
BF16 = jnp.bfloat16

GRID_W = 64
HEAD_DIM = 64
ROPE_BASE = 10000.0
EPS = 1e-6
NA_HEADS = 8
WIN_R = 8
WIN_C = 16
DF_HEADS = 4
MLA_HEADS = 4
Q_RANK = 384
KV_RANK = 128
NOPE_DIM = 128
ROPE_DIM = 64
V_DIM = 128
MLA_QK = NOPE_DIM + ROPE_DIM
SSD_HEADS = 8
SSD_HEAD_DIM = 64
SSD_GROUPS = 2
D_STATE = 128
CHUNK = 128
TOP_K = 4
SWIGLU_ALPHA = 1.702
SWIGLU_LIMIT = 7.0
N_BRANCH = 4
BRANCH_W = 512

V7X_VMEM_BYTES = 64 * 1024 * 1024
VMEM_LIMIT = V7X_VMEM_BYTES - 8 * 1024 * 1024
LANE = 128

NA_ROWS_PER_STEP = 4
MOE_TOKENS = 512
MOE_FF_CHUNK = 512
MOE_OUT_CHUNK = 512


def _cparams(*sem):
    return pltpu.CompilerParams(dimension_semantics=sem, vmem_limit_bytes=VMEM_LIMIT)


def _tile(n, pref):
    t = min(n, pref)
    while n % t:
        t //= 2
    return t


def _dot(a, b):
    return jnp.dot(a, b, preferred_element_type=F32)


def _dot_nt(a, b):
    return lax.dot_general(a, b, (((1,), (1,)), ((), ())), preferred_element_type=F32)


def _dot_tn(a, b):
    return lax.dot_general(a, b, (((0,), (0,)), ((), ())), preferred_element_type=F32)


def _split3(x):
    h = x.astype(BF16)
    r = x - h.astype(F32)
    m = r.astype(BF16)
    lo = (r - m.astype(F32)).astype(BF16)
    return h, m, lo


def _sel_dot(x, sel):
    h, m, lo = _split3(x)
    return _dot(h, sel) + _dot(m, sel) + _dot(lo, sel)


def _dot_hi(a, b):
    a1, a2, _ = _split3(a)
    b1, b2, _ = _split3(b)
    return _dot(a1, b1) + (_dot(a1, b2) + _dot(a2, b1))


def _rope_rotate(y, cos, sin):
    w = y.shape[-1]
    lane = lax.broadcasted_iota(jnp.int32, y.shape, 1)
    partner = jnp.where((lane % 32) < 16, pltpu.roll(y, w - 16, 1), pltpu.roll(y, 16, 1))
    return y * cos + partner * sin


def _rope_tables(t_len, width):
    half = HEAD_DIM // 2
    inv = jnp.asarray(ROPE_BASE ** (-np.arange(0, half, 2) / half), F32)
    t = np.arange(t_len)
    cols = []
    sins = []
    for pos in (t // GRID_W, t % GRID_W):
        ang = jnp.asarray(pos, F32)[:, None] * inv[None, :]
        c, s = jnp.cos(ang), jnp.sin(ang)
        cols += [c, c]
        sins += [-s, s]
    cos = jnp.concatenate(cols, axis=-1)
    sin = jnp.concatenate(sins, axis=-1)
    rep = width // HEAD_DIM
    return jnp.tile(cos, (1, rep)), jnp.tile(sin, (1, rep))


def _group_selector(width, group):
    g = np.arange(width) // group
    return jnp.asarray(g[:, None] == g[None, :], BF16)


def _mod_kernel(c_ref, w_ref, b_ref, o_ref):
    c = c_ref[...]
    s = (c * jax.nn.sigmoid(c)).astype(BF16)
    o_ref[0] = _dot(s, w_ref[0].astype(BF16)) + b_ref[0]


def _mod_params(cc, w_mod, b_mod):
    depth, d, n = w_mod.shape
    rows = cc.shape[0]
    tn = _tile(n, 1024)
    return pl.pallas_call(
        _mod_kernel,
        grid=(depth, n // tn),
        in_specs=[pl.BlockSpec((rows, d), lambda l, j: (0, 0)),
                  pl.BlockSpec((1, d, tn), lambda l, j: (l, 0, j)),
                  pl.BlockSpec((1, 1, tn), lambda l, j: (l, 0, j))],
        out_specs=pl.BlockSpec((1, rows, tn), lambda l, j: (l, 0, j)),
        out_shape=jax.ShapeDtypeStruct((depth, rows, n), F32),
        compiler_params=_cparams("parallel", "parallel"),
    )(cc, w_mod, b_mod.reshape(depth, 1, n))


def _norm_mod_body(x_ref, g_ref, sh_ref, sc_ref):
    x = x_ref[0]
    y = x * lax.rsqrt(jnp.mean(x * x, axis=-1, keepdims=True) + EPS)
    y = y * g_ref[...]
    return y * (1.0 + sc_ref[0]) + sh_ref[0]


def _norm_mod_kernel(x_ref, g_ref, sh_ref, sc_ref, o_ref):
    o_ref[0] = _norm_mod_body(x_ref, g_ref, sh_ref, sc_ref).astype(o_ref.dtype)


def _norm_mod_router_kernel(x_ref, g_ref, sh_ref, sc_ref, wr_ref, br_ref, o_ref, lg_ref):
    y = _norm_mod_body(x_ref, g_ref, sh_ref, sc_ref)
    o_ref[0] = y.astype(o_ref.dtype)
    lg_ref[0] = _dot_hi(y, wr_ref[...]) + br_ref[...]


def _mod_spec(m, d):
    if m.shape[0] == 1:
        return pl.BlockSpec((1, 1, d), lambda b, t: (0, 0, 0))
    return pl.BlockSpec((1, 1, d), lambda b, t: (b, 0, 0))


def _norm_mod(x, g, shift, scale, router=None):
    bsz, t_len, d = x.shape
    tt = _tile(t_len, 256)
    in_specs = [pl.BlockSpec((1, tt, d), lambda b, t: (b, t, 0)),
                pl.BlockSpec((1, d), lambda b, t: (0, 0)),
                _mod_spec(shift, d), _mod_spec(scale, d)]
    out_spec = pl.BlockSpec((1, tt, d), lambda b, t: (b, t, 0))
    out_shape = jax.ShapeDtypeStruct((bsz, t_len, d), BF16 if router is None else F32)
    args = [x, g.reshape(1, d), shift, scale]
    if router is None:
        return pl.pallas_call(
            _norm_mod_kernel, grid=(bsz, t_len // tt), in_specs=in_specs, out_specs=out_spec,
            out_shape=out_shape, compiler_params=_cparams("parallel", "parallel"))(*args)
    wr, br = router
    ne = wr.shape[1]
    in_specs += [pl.BlockSpec((d, ne), lambda b, t: (0, 0)), pl.BlockSpec((1, ne), lambda b, t: (0, 0))]
    return pl.pallas_call(
        _norm_mod_router_kernel, grid=(bsz, t_len // tt), in_specs=in_specs,
        out_specs=[out_spec, pl.BlockSpec((1, tt, ne), lambda b, t: (b, t, 0))],
        out_shape=[out_shape, jax.ShapeDtypeStruct((bsz, t_len, ne), F32)],
        compiler_params=_cparams("parallel", "parallel"))(*args, wr, br)


def _mm_kernel(x_ref, w_ref, o_ref):
    o_ref[0] = _dot(x_ref[0].astype(BF16), w_ref[...].astype(BF16)).astype(o_ref.dtype)


def _mm_res_kernel(x_ref, w_ref, r_ref, g_ref, o_ref):
    o_ref[0] = r_ref[0] + g_ref[0] * _dot(x_ref[0].astype(BF16), w_ref[...].astype(BF16))


def _w_spec(w, layer, k, tn):
    if w.ndim == 3:
        return pl.BlockSpec((None, k, tn), lambda n, b, t: (layer, 0, n))
    return pl.BlockSpec((k, tn), lambda n, b, t: (0, n))


def _mm(x, w, layer=0, out_dtype=F32, tm_pref=1024, tn_pref=512):
    bsz, t_len, k = x.shape
    n = w.shape[-1]
    tm = _tile(t_len, tm_pref)
    tn = n if n % LANE else _tile(n, tn_pref)
    return pl.pallas_call(
        _mm_kernel,
        grid=(n // tn, bsz, t_len // tm),
        in_specs=[pl.BlockSpec((1, tm, k), lambda n_, b, t: (b, t, 0)), _w_spec(w, layer, k, tn)],
        out_specs=pl.BlockSpec((1, tm, tn), lambda n_, b, t: (b, t, n_)),
        out_shape=jax.ShapeDtypeStruct((bsz, t_len, n), out_dtype),
        compiler_params=_cparams("parallel", "parallel", "parallel"),
    )(x, w)


def _mm_residual(x, w, layer, res, gate):
    bsz, t_len, k = x.shape
    n = w.shape[-1]
    tm = _tile(t_len, 1024)
    tn = _tile(n, 512)
    if gate.shape[0] == 1:
        g_spec = pl.BlockSpec((1, 1, tn), lambda n_, b, t: (0, 0, n_))
    else:
        g_spec = pl.BlockSpec((1, 1, tn), lambda n_, b, t: (b, 0, n_))
    return pl.pallas_call(
        _mm_res_kernel,
        grid=(n // tn, bsz, t_len // tm),
        in_specs=[pl.BlockSpec((1, tm, k), lambda n_, b, t: (b, t, 0)), _w_spec(w, layer, k, tn),
                  pl.BlockSpec((1, tm, tn), lambda n_, b, t: (b, t, n_)), g_spec],
        out_specs=pl.BlockSpec((1, tm, tn), lambda n_, b, t: (b, t, n_)),
        out_shape=jax.ShapeDtypeStruct((bsz, t_len, n), F32),
        compiler_params=_cparams("parallel", "parallel", "parallel"),
    )(x, w, res, gate)


QKV_GROUPS = 6
QKV_VALUE_GROUPS = (2, 5)
QKV_ROPE_GROUPS = (3, 4)


def _qkv_kernel(*refs, rope):
    if rope:
        x_ref, w_ref, g_ref, e_ref, cos_ref, sin_ref, o_ref = refs
    else:
        x_ref, w_ref, g_ref, e_ref, o_ref = refs
    n = pl.program_id(0)
    y = _dot(x_ref[0], w_ref[...])
    is_value = functools.reduce(jnp.logical_or, [n == v for v in QKV_VALUE_GROUPS])

    @pl.when(is_value)
    def _():
        o_ref[0] = y

    @pl.when(jnp.logical_not(is_value))
    def _():
        ms = _sel_dot(y * y, e_ref[...]) * (1.0 / HEAD_DIM)
        yn = y * lax.rsqrt(ms + EPS) * g_ref[0]
        if rope:
            is_rope = functools.reduce(jnp.logical_or, [n == v for v in QKV_ROPE_GROUPS])

            @pl.when(is_rope)
            def _():
                o_ref[0] = _rope_rotate(yn, cos_ref[...], sin_ref[...])

            @pl.when(jnp.logical_not(is_rope))
            def _():
                o_ref[0] = yn
        else:
            o_ref[0] = yn


def _qkv_proj(h, w_qkv, gains, rope=None):
    bsz, t_len, k = h.shape
    w = BRANCH_W
    tm = _tile(t_len, 1024)
    in_specs = [pl.BlockSpec((1, tm, k), lambda n, b, t: (b, t, 0)),
                pl.BlockSpec((k, w), lambda n, b, t: (0, n)),
                pl.BlockSpec((1, 1, w), lambda n, b, t: (n, 0, 0)),
                pl.BlockSpec((w, w), lambda n, b, t: (0, 0))]
    args = [h, w_qkv, gains, _group_selector(w, HEAD_DIM)]
    if rope is not None:
        in_specs += [pl.BlockSpec((tm, w), lambda n, b, t: (t, 0))] * 2
        args += list(rope)
    return pl.pallas_call(
        functools.partial(_qkv_kernel, rope=rope is not None),
        grid=(QKV_GROUPS, bsz, t_len // tm), in_specs=in_specs,
        out_specs=pl.BlockSpec((1, tm, w), lambda n, b, t: (b, t, n)),
        out_shape=jax.ShapeDtypeStruct((bsz, t_len, QKV_GROUPS * w), F32),
        compiler_params=_cparams("parallel", "parallel", "parallel"),
    )(*args)


def _attn_kernel(*refs, heads, maps, pieces, dv, scale, has_ctx, diff, post_scale):
    it = iter(refs)
    q_ref, k_ref, v_ref = next(it), next(it), next(it)
    kc_ref = vc_ref = None
    if has_ctx:
        kc_ref, vc_ref = next(it), next(it)
    if diff is not None:
        lq1, lk1, lq2, lk2, gsub_ref = next(it), next(it), next(it), next(it), next(it)
    o_ref = next(it)

    q = q_ref[0].astype(BF16)
    k = k_ref[0].astype(BF16)
    v = v_ref[0].astype(BF16)
    if has_ctx:
        kc = kc_ref[0, 0].astype(BF16)
        vc = vc_ref[0, 0].astype(BF16)
    if diff is not None:
        lam = (jnp.exp(jnp.sum(lq1[...] * lk1[...], axis=-1, keepdims=True))
               - jnp.exp(jnp.sum(lq2[...] * lk2[...], axis=-1, keepdims=True)) + diff)

    for h in range(heads):
        w_own = w_ctx = None
        for m in range(maps):
            s = s_c = None
            for (qo, ko, d) in pieces(h, m):
                part = _dot_nt(q[:, qo:qo + d], k[:, ko:ko + d])
                s = part if s is None else s + part
                if has_ctx:
                    part_c = _dot_nt(q[:, qo:qo + d], kc[:, ko:ko + d])
                    s_c = part_c if s_c is None else s_c + part_c
            s = s * scale
            mx = jnp.max(s, axis=-1, keepdims=True)
            if has_ctx:
                s_c = s_c * scale
                mx = jnp.maximum(mx, jnp.max(s_c, axis=-1, keepdims=True))
            p = jnp.exp(s - mx)
            den = jnp.sum(p, axis=-1, keepdims=True)
            if has_ctx:
                p_c = jnp.exp(s_c - mx)
                den = den + jnp.sum(p_c, axis=-1, keepdims=True)
            inv = 1.0 / den
            if m == 1:
                inv = -lam * inv
            w_own = p * inv if w_own is None else w_own + p * inv
            if has_ctx:
                w_ctx = p_c * inv if w_ctx is None else w_ctx + p_c * inv
        o = _dot(w_own.astype(BF16), v[:, h * dv:(h + 1) * dv])
        if has_ctx:
            o = o + _dot(w_ctx.astype(BF16), vc[:, h * dv:(h + 1) * dv])
        if diff is not None:
            o = o * lax.rsqrt(jnp.mean(o * o, axis=-1, keepdims=True) + EPS) * gsub_ref[...] * post_scale
        o_ref[0, :, h * dv:(h + 1) * dv] = o.astype(o_ref.dtype)


def _attention(q, k, v, v_col, *, heads, maps, pieces, dv, scale, ctx=None, diff=None, q_col=0, k_col=0,
               qk_width=None):
    bsz, t_len, _ = q.shape
    wq = wk = qk_width if qk_width is not None else q.shape[-1]
    wv = heads * dv
    tq = _tile(t_len, 256)
    in_specs = [pl.BlockSpec((1, tq, wq), lambda b, t: (b, t, q_col)),
                pl.BlockSpec((1, t_len, wk), lambda b, t: (b, 0, k_col)),
                pl.BlockSpec((1, t_len, wv), lambda b, t: (b, 0, v_col))]
    args = [q, k, v]
    if ctx is not None:
        kc, vc, layer, vc_col = ctx
        tc = kc.shape[2]
        in_specs += [pl.BlockSpec((1, 1, tc, wk), lambda b, t: (b, layer, 0, 0)),
                     pl.BlockSpec((1, 1, tc, wv), lambda b, t: (b, layer, 0, vc_col))]
        args += [kc, vc]
    lam_init = 0.0
    if diff is not None:
        lq1, lk1, lq2, lk2, gsub, lam_init = diff
        for a in (lq1, lk1, lq2, lk2, gsub):
            in_specs.append(pl.BlockSpec((1, a.shape[0]), lambda b, t: (0, 0)))
            args.append(a.reshape(1, -1))
    kern = functools.partial(
        _attn_kernel, heads=heads, maps=maps, pieces=pieces, dv=dv, scale=scale,
        has_ctx=ctx is not None, diff=lam_init if diff is not None else None,
        post_scale=1.0 - lam_init)
    return pl.pallas_call(
        kern, grid=(bsz, t_len // tq), in_specs=in_specs,
        out_specs=pl.BlockSpec((1, tq, wv), lambda b, t: (b, t, 0)),
        out_shape=jax.ShapeDtypeStruct((bsz, t_len, wv), BF16),
        compiler_params=_cparams("parallel", "parallel"),
    )(*args)


def _na_pieces(h, m):
    return [(h * HEAD_DIM, h * HEAD_DIM, HEAD_DIM)]


def _df_pieces(h, m):
    o = (2 * h + m) * HEAD_DIM
    return [(o, o, HEAD_DIM)]


def _mla_pieces(h, m):
    nope_w = MLA_HEADS * NOPE_DIM
    return [(h * NOPE_DIM, h * NOPE_DIM, NOPE_DIM),
            (nope_w + h * ROPE_DIM, nope_w + h * ROPE_DIM, ROPE_DIM)]


def _na_latent_kernel(q_ref, k_ref, v_ref, kc_ref, vc_ref, bias_ref, o_ref, *, rows, kr, rps, scale):
    kc = kc_ref[0, 0].astype(BF16)
    vc = vc_ref[0, 0].astype(BF16)
    for rr in range(rps):
        r = pl.program_id(1) * rps + rr
        start_row = jnp.clip(r - WIN_R // 2, 0, rows - kr)
        start = pl.multiple_of(start_row * GRID_W, GRID_W)
        q = q_ref[0, rr * GRID_W:(rr + 1) * GRID_W, :].astype(BF16)
        kl = k_ref[0, pl.ds(start, kr * GRID_W), :].astype(BF16)
        vl = v_ref[0, pl.ds(start, kr * GRID_W), :].astype(BF16)
        ro0 = start_row - r + WIN_R - 1
        s_heads = []
        for h in range(NA_HEADS):
            sl = slice(h * HEAD_DIM, (h + 1) * HEAD_DIM)
            bias = jnp.concatenate([bias_ref[h, ro0 + 2 * j] for j in range(kr // 2)], axis=-1)
            s_loc = _dot_nt(q[:, sl], kl[:, sl]) * scale + bias
            s_ctx = _dot_nt(q[:, sl], kc[:, sl]) * scale
            s_heads.append(jnp.concatenate([s_loc, s_ctx], axis=-1))
        s = jnp.concatenate(s_heads, axis=0)
        p = jnp.exp(s - jnp.max(s, axis=-1, keepdims=True))
        pn = (p * (1.0 / jnp.sum(p, axis=-1, keepdims=True))).astype(BF16)
        n_loc = kr * GRID_W
        for h in range(NA_HEADS):
            sl = slice(h * HEAD_DIM, (h + 1) * HEAD_DIM)
            ph = pn[h * GRID_W:(h + 1) * GRID_W]
            o = _dot(ph[:, :n_loc], vl[:, sl]) + _dot(ph[:, n_loc:], vc[:, sl])
            o_ref[0, rr * GRID_W:(rr + 1) * GRID_W, sl] = o.astype(o_ref.dtype)


def _na_bias_table(rpb_l):
    qc = np.arange(GRID_W)
    kc = np.arange(GRID_W)
    qstart = np.clip(qc - WIN_C // 2, 0, GRID_W - WIN_C)
    valid = (kc[None, :] >= qstart[:, None]) & (kc[None, :] < qstart[:, None] + WIN_C)
    c_off = np.clip(kc[None, :] - qc[:, None] + WIN_C - 1, 0, 2 * WIN_C - 2)
    onehot = (c_off[None] == np.arange(2 * WIN_C - 1)[:, None, None]) & valid[None]
    tab = jnp.einsum('hrc,cqk->hrqk', rpb_l.astype(F32), jnp.asarray(onehot, F32),
                     precision=lax.Precision.HIGHEST)
    tab = jnp.where(valid[None, None], tab, -jnp.inf)
    return jnp.concatenate([tab[:, :-1], tab[:, 1:]], axis=-1)


def _na_latent(q, k, v, v_col, kc, vc, layer, rpb_l, q_col=0, k_col=0):
    bsz, t_len, _ = q.shape
    w = NA_HEADS * HEAD_DIM
    rows = t_len // GRID_W
    kr = min(WIN_R, rows)
    assert kr % 2 == 0, "the bias table pairs key rows"
    tc = kc.shape[2]
    bias = _na_bias_table(rpb_l)
    rps = _tile(rows, NA_ROWS_PER_STEP)
    return pl.pallas_call(
        functools.partial(_na_latent_kernel, rows=rows, kr=kr, rps=rps, scale=HEAD_DIM ** -0.5),
        grid=(bsz, rows // rps),
        in_specs=[pl.BlockSpec((1, rps * GRID_W, w), lambda b, r: (b, r, q_col)),
                  pl.BlockSpec((1, t_len, w), lambda b, r: (b, 0, k_col)),
                  pl.BlockSpec((1, t_len, w), lambda b, r: (b, 0, v_col)),
                  pl.BlockSpec((1, 1, tc, w), lambda b, r: (b, layer, 0, 0)),
                  pl.BlockSpec((1, 1, tc, w), lambda b, r: (b, layer, 0, 0)),
                  _const_spec(bias)],
        out_specs=pl.BlockSpec((1, rps * GRID_W, w), lambda b, r: (b, r, 0)),
        out_shape=jax.ShapeDtypeStruct((bsz, t_len, w), BF16),
        compiler_params=_cparams("parallel", "parallel"),
    )(q, k, v, kc, vc, bias)


def _mla_k_part(ckv_bf16, krope, wukv_ref, gk_ref, ek_ref, tile_ref, rope_tabs):
    nope_w = MLA_HEADS * NOPE_DIM
    kv = _dot(ckv_bf16, wukv_ref[...])
    k_nope = kv[:, :nope_w]
    kr_t = _sel_dot(krope, tile_ref[...])
    sq = jnp.concatenate([k_nope * k_nope, krope * krope], axis=-1)
    ms = _sel_dot(sq, ek_ref[...]) * (1.0 / MLA_QK)
    rs = lax.rsqrt(ms + EPS)
    g = gk_ref[...]
    kn = k_nope * rs[:, :nope_w] * g[:, :nope_w]
    kro = kr_t * rs[:, nope_w:] * g[:, nope_w:]
    if rope_tabs is not None:
        kro = _rope_rotate(kro, rope_tabs[0][...], rope_tabs[1][...])
    return jnp.concatenate([kn, kro], axis=-1), kv[:, nope_w:]


def _mla_prep_kernel(*refs, rope):
    it = iter(refs)
    x_ref = next(it)
    gcq_ref, gckv_ref, wuq_ref, wukv_ref, gq_ref, gk_ref, eq_ref, ek_ref, tile_ref = (next(it) for _ in range(9))
    tabs = (next(it), next(it)) if rope else None
    q_ref, k_ref, v_ref, ckv_ref, kr_ref = (next(it) for _ in range(5))
    nope_w = MLA_HEADS * NOPE_DIM

    x = x_ref[0]
    dq = x[:, :Q_RANK]
    cq = dq * lax.rsqrt(jnp.mean(dq * dq, axis=-1, keepdims=True) + EPS) * gcq_ref[...]
    dkv = x[:, Q_RANK:Q_RANK + KV_RANK]
    ckv = dkv * lax.rsqrt(jnp.mean(dkv * dkv, axis=-1, keepdims=True) + EPS) * gckv_ref[...]
    krope = x[:, Q_RANK + KV_RANK:]
    ckv_ref[0] = ckv
    kr_ref[0] = krope

    qr = _dot(cq.astype(BF16), wuq_ref[...])
    ms = _sel_dot(qr * qr, eq_ref[...]) * (1.0 / MLA_QK)
    qn = qr * lax.rsqrt(ms + EPS) * gq_ref[...]
    if rope:
        q_ref[0] = jnp.concatenate(
            [qn[:, :nope_w], _rope_rotate(qn[:, nope_w:], tabs[0][...], tabs[1][...])], axis=-1)
    else:
        q_ref[0] = qn
    kn, v = _mla_k_part(ckv.astype(BF16), krope, wukv_ref, gk_ref, ek_ref, tile_ref, tabs)
    k_ref[0] = kn
    v_ref[0] = v.astype(v_ref.dtype)


def _mla_ctx_kernel(ckv_ref, kr_ref, wukv_ref, gk_ref, ek_ref, tile_ref, k_ref, v_ref):
    kn, v = _mla_k_part(ckv_ref[0, 0].astype(BF16), kr_ref[0, 0], wukv_ref, gk_ref, ek_ref, tile_ref, None)
    k_ref[0, 0] = kn
    v_ref[0, 0] = v.astype(v_ref.dtype)


def _mla_consts():
    nope_w = MLA_HEADS * NOPE_DIM
    qk_w = nope_w + MLA_HEADS * ROPE_DIM
    head_q = np.concatenate([np.arange(nope_w) // NOPE_DIM, np.arange(MLA_HEADS * ROPE_DIM) // ROPE_DIM])
    eq = head_q[:, None] == head_q[None, :]
    ek = np.concatenate([eq[:nope_w], np.ones((ROPE_DIM, qk_w), bool)], axis=0)
    tile = np.arange(ROPE_DIM)[:, None] == (np.arange(MLA_HEADS * ROPE_DIM) % ROPE_DIM)[None, :]
    return jnp.asarray(eq, BF16), jnp.asarray(ek, BF16), jnp.asarray(tile, BF16)


def _mla_perm_weights(w_uq_l, w_ukv_l, g_q, g_k):
    wq = w_uq_l.reshape(Q_RANK, MLA_HEADS, MLA_QK)
    wq = jnp.concatenate([wq[:, :, :NOPE_DIM].reshape(Q_RANK, -1), wq[:, :, NOPE_DIM:].reshape(Q_RANK, -1)], axis=1)
    wkv = w_ukv_l.reshape(KV_RANK, MLA_HEADS, NOPE_DIM + V_DIM)
    wkv = jnp.concatenate([wkv[:, :, :NOPE_DIM].reshape(KV_RANK, -1), wkv[:, :, NOPE_DIM:].reshape(KV_RANK, -1)], axis=1)

    def gain(g):
        return jnp.concatenate([jnp.tile(g[:NOPE_DIM], MLA_HEADS), jnp.tile(g[NOPE_DIM:], MLA_HEADS)]).reshape(1, -1)

    return wq.astype(BF16), wkv.astype(BF16), gain(g_q), gain(g_k)


def _const_spec(a):
    nd = a.ndim
    return pl.BlockSpec(a.shape, lambda *_: (0,) * nd)


def _mla_prep(x, g_cq, g_ckv, wq, wkv, gq, gk, consts, rope):
    bsz, t_len, w = x.shape
    tt = _tile(t_len, 256)
    eq, ek, tile = consts
    qk_w = MLA_HEADS * MLA_QK
    args = [x, g_cq.reshape(1, -1), g_ckv.reshape(1, -1), wq, wkv, gq, gk, eq, ek, tile]
    in_specs = [pl.BlockSpec((1, tt, w), lambda b, t: (b, t, 0))] + [_const_spec(a) for a in args[1:]]
    if rope is not None:
        rw = MLA_HEADS * ROPE_DIM
        in_specs += [pl.BlockSpec((tt, rw), lambda b, t: (t, 0))] * 2
        args += [rope[0][:, :rw], rope[1][:, :rw]]
    widths = (qk_w, qk_w, MLA_HEADS * V_DIM, KV_RANK, ROPE_DIM)
    dtypes = (F32, F32, BF16, F32, F32)
    return pl.pallas_call(
        functools.partial(_mla_prep_kernel, rope=rope is not None),
        grid=(bsz, t_len // tt), in_specs=in_specs,
        out_specs=[pl.BlockSpec((1, tt, wd), lambda b, t: (b, t, 0)) for wd in widths],
        out_shape=[jax.ShapeDtypeStruct((bsz, t_len, wd), dt) for wd, dt in zip(widths, dtypes)],
        compiler_params=_cparams("parallel", "parallel"),
    )(*args)


def _mla_ctx(cache_ckv, cache_krope, layer, wkv, gk, consts):
    bsz, depth, tc, _ = cache_ckv.shape
    _, ek, tile = consts
    qk_w = MLA_HEADS * MLA_QK
    vw = MLA_HEADS * V_DIM
    args = [cache_ckv, cache_krope, wkv, gk, ek, tile]
    in_specs = [pl.BlockSpec((1, 1, tc, KV_RANK), lambda b: (b, layer, 0, 0)),
                pl.BlockSpec((1, 1, tc, ROPE_DIM), lambda b: (b, layer, 0, 0))]
    in_specs += [_const_spec(a) for a in args[2:]]
    return pl.pallas_call(
        _mla_ctx_kernel, grid=(bsz,), in_specs=in_specs,
        out_specs=[pl.BlockSpec((1, 1, tc, qk_w), lambda b: (b, 0, 0, 0)),
                   pl.BlockSpec((1, 1, tc, vw), lambda b: (b, 0, 0, 0))],
        out_shape=[jax.ShapeDtypeStruct((bsz, 1, tc, qk_w), F32), jax.ShapeDtypeStruct((bsz, 1, tc, vw), BF16)],
        compiler_params=_cparams("parallel"),
    )(*args)


def _ssd_prep_kernel(x_ref, cw_ref, cb_ref, dtb_ref, xs_ref, bm_ref, cm_ref, dt_ref):
    t_len = x_ref.shape[1]
    cw_total = cw_ref.shape[1]
    xbc = x_ref[0, :, BRANCH_W:BRANCH_W + cw_total]
    row = lax.broadcasted_iota(jnp.int32, xbc.shape, 0)
    prev1 = jnp.where(row >= 1, pltpu.roll(xbc, 1, 0), 0.0)
    next1 = jnp.where(row < t_len - 1, pltpu.roll(xbc, t_len - 1, 0), 0.0)
    next2 = jnp.where(row < t_len - 2, pltpu.roll(xbc, t_len - 2, 0), 0.0)
    cw = cw_ref[...]
    u = prev1 * cw[0:1] + xbc * cw[1:2] + next1 * cw[2:3] + next2 * cw[3:4] + cb_ref[...]
    u = u * jax.nn.sigmoid(u)
    xw = SSD_HEADS * SSD_HEAD_DIM
    bw = SSD_GROUPS * D_STATE
    xs_ref[0] = u[:, :xw]
    bm_ref[0] = u[:, xw:xw + bw]
    cm_ref[0] = u[:, xw + bw:]
    raw = x_ref[0, :, BRANCH_W + cw_total:] + dtb_ref[...]
    dt_ref[0] = jnp.maximum(raw, 0.0) + jnp.log1p(jnp.exp(-jnp.abs(raw)))


def _ssd_prep(g3, conv_w, conv_b, dt_bias):
    bsz, t_len, w = g3.shape
    cw = conv_w.shape[1]
    xw = SSD_HEADS * SSD_HEAD_DIM
    bw = SSD_GROUPS * D_STATE
    nh2 = 2 * SSD_HEADS
    widths = (xw, bw, bw, nh2)
    return pl.pallas_call(
        _ssd_prep_kernel, grid=(bsz,),
        in_specs=[pl.BlockSpec((1, t_len, w), lambda b: (b, 0, 0)),
                  pl.BlockSpec(conv_w.shape, lambda b: (0, 0)),
                  pl.BlockSpec((1, cw), lambda b: (0, 0)),
                  pl.BlockSpec((1, nh2), lambda b: (0, 0))],
        out_specs=[pl.BlockSpec((1, t_len, wd), lambda b: (b, 0, 0)) for wd in widths],
        out_shape=[jax.ShapeDtypeStruct((bsz, t_len, wd), F32) for wd in widths],
        compiler_params=_cparams("parallel"),
    )(g3, conv_w, conv_b.reshape(1, cw), dt_bias.reshape(1, nh2))


def _ssd_scan_kernel(*refs, has_s0, nc):
    if has_s0:
        xs_ref, bm_ref, cm_ref, dt_ref, dtt_ref, alogt_ref, s0_ref, y_ref, sfin_ref, st_ref = refs
    else:
        xs_ref, bm_ref, cm_ref, dt_ref, dtt_ref, alogt_ref, y_ref, sfin_ref, st_ref = refs
    d = pl.program_id(1)
    c = pl.program_id(2)
    chunk = xs_ref.shape[1]

    @pl.when(c == 0)
    def _():
        if has_s0:
            st_ref[...] = s0_ref[0, 0, 0]
        else:
            st_ref[...] = jnp.zeros_like(st_ref)

    a_col = -jnp.exp(alogt_ref[0])
    dt = dt_ref[0, 0]
    dta_t = dtt_ref[0, 0] * a_col
    ri = lax.broadcasted_iota(jnp.int32, (chunk, chunk), 0)
    ci = lax.broadcasted_iota(jnp.int32, (chunk, chunk), 1)
    sign = 1 - 2 * d
    mask = (ri - ci) * sign >= 0
    mask_t = jnp.where((ci - ri) * sign >= 0, 1.0, 0.0).astype(BF16)
    acs_rows = _sel_dot(dta_t, mask_t)

    xs = xs_ref[0]
    rep = SSD_HEADS // SSD_GROUPS
    cb = []
    for g in range(SSD_GROUPS):
        cg = cm_ref[0, :, g * D_STATE:(g + 1) * D_STATE].astype(BF16)
        bg = bm_ref[0, :, g * D_STATE:(g + 1) * D_STATE].astype(BF16)
        cb.append((cg, bg, _dot_nt(cg, bg)))

    heads = range(SSD_HEADS)
    dta_rows = [dta_t[h:h + 1, :] for h in heads]
    acs_cols = [jnp.sum(jnp.where(mask, dta_rows[h], 0.0), axis=-1, keepdims=True) for h in heads]
    tots = [jnp.sum(dta_rows[h], axis=-1, keepdims=True) for h in heads]
    decays = [jnp.exp(jnp.where(mask, acs_cols[h] - acs_rows[h:h + 1, :], -jnp.inf)) for h in heads]
    xdts = [xs[:, h * SSD_HEAD_DIM:(h + 1) * SSD_HEAD_DIM] * dt[:, h:h + 1] for h in heads]
    states = [st_ref[h] for h in heads]
    y_diag = [_dot((cb[h // rep][2] * decays[h]).astype(BF16), xdts[h].astype(BF16)) for h in heads]
    y_off = [_dot_nt(cb[h // rep][0], states[h].astype(BF16)) for h in heads]
    upd = [_dot_tn((xdts[h] * jnp.exp(tots[h] - acs_cols[h])).astype(BF16), cb[h // rep][1]) for h in heads]
    for h in heads:
        st_ref[h] = states[h] * jnp.exp(tots[h]) + upd[h]
        y_ref[0, 0, :, h * SSD_HEAD_DIM:(h + 1) * SSD_HEAD_DIM] = y_diag[h] + jnp.exp(acs_cols[h]) * y_off[h]

    @pl.when(c == nc - 1)
    def _():
        sfin_ref[0, 0] = st_ref[...]


def _ssd_scan(xs, bm, cm, dt, a_log_l, s0=None, layer=0):
    bsz, t_len, xw = xs.shape
    chunk = CHUNK if t_len % CHUNK == 0 else t_len
    nc = t_len // chunk
    nh = SSD_HEADS
    bw = bm.shape[-1]
    dt4 = dt.reshape(bsz, t_len, 2, nh).transpose(0, 2, 1, 3)
    dtt = dt4.transpose(0, 1, 3, 2)

    def tok(b, d, c):
        return c + d * (nc - 1 - 2 * c)

    in_specs = [pl.BlockSpec((1, chunk, xw), lambda b, d, c: (b, tok(b, d, c), 0)),
                pl.BlockSpec((1, chunk, bw), lambda b, d, c: (b, tok(b, d, c), 0)),
                pl.BlockSpec((1, chunk, bw), lambda b, d, c: (b, tok(b, d, c), 0)),
                pl.BlockSpec((1, 1, chunk, nh), lambda b, d, c: (b, d, tok(b, d, c), 0)),
                pl.BlockSpec((1, 1, nh, chunk), lambda b, d, c: (b, d, 0, tok(b, d, c))),
                pl.BlockSpec((1, nh, 1), lambda b, d, c: (d, 0, 0))]
    args = [xs, bm, cm, dt4, dtt, a_log_l.reshape(2, nh, 1)]
    if s0 is not None:
        in_specs.append(pl.BlockSpec((1, 1, 1, nh, SSD_HEAD_DIM, D_STATE), lambda b, d, c: (b, layer, d, 0, 0, 0)))
        args.append(s0)
    return pl.pallas_call(
        functools.partial(_ssd_scan_kernel, has_s0=s0 is not None, nc=nc),
        grid=(bsz, 2, nc), in_specs=in_specs,
        out_specs=[pl.BlockSpec((1, 1, chunk, xw), lambda b, d, c: (d, b, tok(b, d, c), 0)),
                   pl.BlockSpec((1, 1, nh, SSD_HEAD_DIM, D_STATE), lambda b, d, c: (b, d, 0, 0, 0))],
        out_shape=[jax.ShapeDtypeStruct((2, bsz, t_len, xw), F32),
                   jax.ShapeDtypeStruct((bsz, 2, nh, SSD_HEAD_DIM, D_STATE), F32)],
        scratch_shapes=[pltpu.VMEM((nh, SSD_HEAD_DIM, D_STATE), F32)],
        compiler_params=_cparams("parallel", "arbitrary", "arbitrary"),
    )(*args)


def _ssd_post_kernel(yf_ref, yb_ref, xs_ref, z_ref, dsk_ref, g_ref, o_ref):
    z = z_ref[0]
    y = yf_ref[0, 0] + yb_ref[0, 0] + dsk_ref[...] * xs_ref[0]
    y = y * (z * jax.nn.sigmoid(z))
    y = y * lax.rsqrt(jnp.mean(y * y, axis=-1, keepdims=True) + EPS) * g_ref[...]
    o_ref[0] = y.astype(o_ref.dtype)


def _ssd_post(y2, xs, g3, d_skip, g_ssd):
    bsz, t_len, xw = xs.shape
    tt = _tile(t_len, 256)
    dsk = jnp.repeat(d_skip, SSD_HEAD_DIM).reshape(1, xw)
    return pl.pallas_call(
        _ssd_post_kernel, grid=(bsz, t_len // tt),
        in_specs=[pl.BlockSpec((1, 1, tt, xw), lambda b, t: (0, b, t, 0)),
                  pl.BlockSpec((1, 1, tt, xw), lambda b, t: (1, b, t, 0)),
                  pl.BlockSpec((1, tt, xw), lambda b, t: (b, t, 0)),
                  pl.BlockSpec((1, tt, xw), lambda b, t: (b, t, 0)),
                  pl.BlockSpec((1, xw), lambda b, t: (0, 0)),
                  pl.BlockSpec((1, xw), lambda b, t: (0, 0))],
        out_specs=pl.BlockSpec((1, tt, xw), lambda b, t: (b, t, 0)),
        out_shape=jax.ShapeDtypeStruct((bsz, t_len, xw), BF16),
        compiler_params=_cparams("parallel", "parallel"),
    )(y2, y2, xs, g3, dsk, g_ssd.reshape(1, xw))


def _merge_kernel(h_ref, wg0_ref, wg1_ref, wg2_ref, wg3_ref, ya_ref, yb_ref, yc_ref, yd_ref, wb_ref, o_ref):
    h = h_ref[0]
    acc = None
    wg_refs = (wg0_ref, wg1_ref, wg2_ref, wg3_ref)
    for i, y_ref in enumerate((ya_ref, yb_ref, yc_ref, yd_ref)):
        gate = jax.nn.sigmoid(_dot(h, wg_refs[i][...]))
        term = gate * _dot(y_ref[0], wb_ref[i].astype(BF16))
        acc = term if acc is None else acc + term
    o_ref[0] = acc.astype(o_ref.dtype)


def _merge(h, wg, ys, w_branch, layer):
    bsz, t_len, d = h.shape
    tm = _tile(t_len, 512)
    tn = _tile(d, 512)
    y_spec = pl.BlockSpec((1, tm, BRANCH_W), lambda n, b, t: (b, t, 0))
    npb = d // tn

    def wg_spec(i):
        return pl.BlockSpec((d, tn), lambda n, b, t: (0, i * npb + n))

    return pl.pallas_call(
        _merge_kernel, grid=(npb, bsz, t_len // tm),
        in_specs=[pl.BlockSpec((1, tm, d), lambda n, b, t: (b, t, 0)),
                  wg_spec(0), wg_spec(1), wg_spec(2), wg_spec(3),
                  y_spec, y_spec, y_spec, y_spec,
                  pl.BlockSpec((None, N_BRANCH, BRANCH_W, tn), lambda n, b, t: (layer, 0, 0, n))],
        out_specs=pl.BlockSpec((1, tm, tn), lambda n, b, t: (b, t, n)),
        out_shape=jax.ShapeDtypeStruct((bsz, t_len, d), BF16),
        compiler_params=_cparams("parallel", "parallel", "parallel"),
    )(h, wg, wg, wg, wg, *ys, w_branch)


def _moe_kernel(blk_e_ref, nused_ref, tok_ref, dst_ref, x_hbm, wg_ref, wu_ref, bg_ref, bu_ref,
                wd_ref, bd_ref, y_hbm, xbuf, xb, act, acc, gsem, ssem, *, ts, nf, nblk, dump0):
    i = pl.program_id(0)
    f = pl.program_id(1)
    nu = nused_ref[0]
    slot = i % 2
    rps = ts // nf

    def start_gather(base, s, r):
        pltpu.make_async_copy(x_hbm.at[pl.ds(tok_ref[base + r], 1)], xbuf.at[s, pl.ds(r, 1)], gsem.at[s]).start()

    def start_scatter(base, s, r):
        dst = jnp.where(i == 0, dump0 + r, dst_ref[base + r])
        pltpu.make_async_copy(acc.at[s, pl.ds(r, 1)], y_hbm.at[pl.ds(dst, 1)], ssem.at[s]).start()

    def gather_wait(s):
        pltpu.make_async_copy(x_hbm.at[pl.ds(0, ts)], xbuf.at[s], gsem.at[s]).wait()

    def scatter_wait(s):
        pltpu.make_async_copy(acc.at[s], y_hbm.at[pl.ds(0, ts)], ssem.at[s]).wait()

    prev_base = jnp.maximum(i - 1, 0) * ts
    next_base = jnp.minimum(i + 1, nblk - 1) * ts
    row0 = f * rps

    @pl.when(f == 0)
    def _():
        @pl.when((i >= 1) & (i <= nu + 1))
        def _():
            scatter_wait(slot)

        @pl.when(i == 0)
        def _():
            def body(r, carry):
                start_gather(0, 0, r)
                return carry

            lax.fori_loop(0, ts, body, 0, unroll=8)
            acc[1] = jnp.zeros(acc.shape[1:], acc.dtype)

        @pl.when(i <= nu)
        def _():
            gather_wait(slot)

        @pl.when(i < nu)
        def _():
            xb[...] = xbuf[slot].astype(BF16)
            for r in range(ts):
                start_gather(next_base, 1 - slot, r)
                start_scatter(prev_base, 1 - slot, r)

    @pl.when((i < nu) & (f < nf))
    def _():
        x = xb[...]
        g = _dot(x, wg_ref[...].astype(BF16)) + bg_ref[...]
        u = _dot(x, wu_ref[...].astype(BF16)) + bu_ref[...]
        g = jnp.minimum(g, SWIGLU_LIMIT)
        u = jnp.clip(u, -SWIGLU_LIMIT, SWIGLU_LIMIT)
        a = (g * jax.nn.sigmoid(SWIGLU_ALPHA * g) * (u + 1.0)).astype(BF16)
        fc = a.shape[1]
        for c in range(nf):
            @pl.when(f == c)
            def _(c=c):
                act[:, c * fc:(c + 1) * fc] = a

    @pl.when((i < nu) & (f >= nf))
    def _():
        y = _dot(act[...], wd_ref[...].astype(BF16)) + bd_ref[...]
        dc = y.shape[1]
        for c in range(acc.shape[2] // dc):
            @pl.when(f == nf + c)
            def _(c=c):
                acc[slot, :, c * dc:(c + 1) * dc] = y

    @pl.when((i == nu) & (f < nf))
    def _():
        def body(k, carry):
            start_scatter(prev_base, 1 - slot, row0 + k)
            return carry

        lax.fori_loop(0, rps, body, 0, unroll=8)


def _moe_blocks(h2, blk_e, nused, slot_tok, slot_dst, n_rows_out, w_gu, b_gu, w_dn, b_dn, layer):
    d = h2.shape[1]
    n_exp, _, ff2 = w_gu.shape[1:]
    ff = ff2 // 2
    ts, fc = MOE_TOKENS, _tile(ff, MOE_FF_CHUNK)
    nf = ff // fc
    nblk = slot_tok.shape[0] // ts

    dc = _tile(d, MOE_OUT_CHUNK)
    nd = d // dc

    def fcl(i, f, nu):
        return jnp.where(i < nu[0], jnp.minimum(f, nf - 1), nf - 1)

    def dcl(i, f, nu):
        return jnp.where(i < nu[0], jnp.maximum(f - nf, 0), nd - 1)

    grid_spec = pltpu.PrefetchScalarGridSpec(
        num_scalar_prefetch=4, grid=(nblk + 2, nf + nd),
        in_specs=[
            pl.BlockSpec(memory_space=pl.ANY),
            pl.BlockSpec((None, None, d, fc), lambda i, f, be, nu, *_: (layer, be[i], 0, fcl(i, f, nu))),
            pl.BlockSpec((None, None, d, fc), lambda i, f, be, nu, *_: (layer, be[i], 0, nf + fcl(i, f, nu))),
            pl.BlockSpec((None, None, 1, fc), lambda i, f, be, nu, *_: (layer, be[i], 0, fcl(i, f, nu))),
            pl.BlockSpec((None, None, 1, fc), lambda i, f, be, nu, *_: (layer, be[i], 0, nf + fcl(i, f, nu))),
            pl.BlockSpec((None, None, ff, dc), lambda i, f, be, nu, *_: (layer, be[i], 0, dcl(i, f, nu))),
            pl.BlockSpec((None, None, 1, dc), lambda i, f, be, nu, *_: (layer, be[i], 0, dcl(i, f, nu))),
        ],
        out_specs=pl.BlockSpec(memory_space=pl.ANY),
        scratch_shapes=[pltpu.VMEM((2, ts, d), F32), pltpu.VMEM((ts, d), BF16), pltpu.VMEM((ts, ff), BF16),
                        pltpu.VMEM((2, ts, d), F32),
                        pltpu.SemaphoreType.DMA((2,)), pltpu.SemaphoreType.DMA((2,))],
    )
    depth = w_gu.shape[0]
    return pl.pallas_call(
        functools.partial(_moe_kernel, ts=ts, nf=nf, nblk=nblk, dump0=n_rows_out - ts), grid_spec=grid_spec,
        out_shape=jax.ShapeDtypeStruct((n_rows_out, d), F32),
        compiler_params=_cparams("arbitrary", "arbitrary"),
    )(blk_e, nused, slot_tok, slot_dst, h2, w_gu, w_gu, b_gu.reshape(depth, n_exp, 1, ff2),
      b_gu.reshape(depth, n_exp, 1, ff2), w_dn, b_dn.reshape(depth, n_exp, 1, d))


def _moe(h2, logits, w_gu, b_gu, w_dn, b_dn, layer):
    n_tok, d = h2.shape
    n_exp = logits.shape[1]
    ts = MOE_TOKENS
    top_v, top_i = lax.top_k(logits, TOP_K)
    gate_w = jax.nn.softmax(top_v, axis=-1)
    n_asg = n_tok * TOP_K
    flat_e = top_i.reshape(-1)
    onehot = (flat_e[:, None] == jnp.arange(n_exp, dtype=flat_e.dtype)[None, :]).astype(jnp.int32)
    counts = jnp.sum(onehot, axis=0)
    padded = (counts + ts - 1) // ts * ts
    pad_ends = jnp.cumsum(padded)
    pad_starts = pad_ends - padded
    pos = jnp.sum(onehot * (jnp.cumsum(onehot, axis=0) - 1 + pad_starts[None, :]), axis=1).astype(jnp.int32)
    nblk = -(-n_asg // ts) + n_exp
    cap = nblk * ts
    slot_asg = jnp.full((cap,), -1, jnp.int32).at[pos].set(jnp.arange(n_asg, dtype=jnp.int32))
    valid = slot_asg >= 0
    slot_tok = jnp.where(valid, slot_asg // TOP_K, 0)
    slot_dst = jnp.where(valid, (slot_asg % TOP_K) * n_tok + slot_asg // TOP_K,
                         n_asg + jnp.arange(cap, dtype=jnp.int32) % ts)
    nused = (pad_ends[-1] // ts).astype(jnp.int32)
    blk = jnp.arange(nblk + 2, dtype=jnp.int32)
    blk_e = jnp.sum((pad_ends[None, :] <= (blk * ts)[:, None]).astype(jnp.int32), axis=1)
    blk_e = jnp.minimum(blk_e, n_exp - 1)
    last_e = jnp.sum(jnp.where(blk == nused - 1, blk_e, 0))
    blk_e = jnp.where(blk < nused, blk_e, last_e).astype(jnp.int32)
    y4 = _moe_blocks(h2, blk_e, nused.reshape(1), slot_tok, slot_dst, n_asg + ts, w_gu, b_gu, w_dn, b_dn, layer)
    return y4, gate_w


def _combine_kernel(x_ref, g_ref, w_ref, y0_ref, y1_ref, y2_ref, y3_ref, o_ref):
    w = w_ref[...]
    m = None
    for j, y_ref in enumerate((y0_ref, y1_ref, y2_ref, y3_ref)):
        term = w[:, j:j + 1] * y_ref[...]
        m = term if m is None else m + term
    o_ref[0] = x_ref[0] + g_ref[0] * m


def _moe_combine(x, gate, y4, gate_w, row0):
    bsz, t_len, d = x.shape
    n_tok = gate_w.shape[0]
    tt = _tile(t_len, 256)
    tpb = t_len // tt
    t0 = row0 // tt
    nt = n_tok // tt

    def y_spec(j):
        return pl.BlockSpec((tt, d), lambda b, t: (j * nt + t0 + b * tpb + t, 0))

    if gate.shape[0] == 1:
        g_spec = pl.BlockSpec((1, 1, d), lambda b, t: (0, 0, 0))
    else:
        g_spec = pl.BlockSpec((1, 1, d), lambda b, t: (b, 0, 0))
    return pl.pallas_call(
        _combine_kernel, grid=(bsz, tpb),
        in_specs=[pl.BlockSpec((1, tt, d), lambda b, t: (b, t, 0)), g_spec,
                  pl.BlockSpec((tt, TOP_K), lambda b, t: (t0 + b * tpb + t, 0)),
                  y_spec(0), y_spec(1), y_spec(2), y_spec(3)],
        out_specs=pl.BlockSpec((1, tt, d), lambda b, t: (b, t, 0)),
        out_shape=jax.ShapeDtypeStruct((bsz, t_len, d), F32),
        compiler_params=_cparams("parallel", "parallel"),
    )(x, gate, gate_w, y4, y4, y4, y4)


def _token_mixers(h, seq, lw, layer, lam_init, ctx):
    latent = ctx is not None
    bsz, t_len = seq
    flat = h.shape[:2]
    rope = _rope_tables(t_len, BRANCH_W) if latent else None
    qkv = _qkv_proj(h, lw['w_qkv'], lw['qkv_gains'], rope).reshape(bsz, t_len, -1)
    g2 = _mm(h, lw['w_lat']).reshape(bsz, t_len, -1)
    g3 = _mm(h, lw['w_ssd']).reshape(bsz, t_len, -1)

    if latent:
        ya = _na_latent(qkv, qkv, qkv, 2, ctx['na_k'], ctx['na_v'], layer, lw['rpb'], q_col=0, k_col=1)
    else:
        ya = _attention(qkv, qkv, qkv, 2, heads=NA_HEADS, maps=1, pieces=_na_pieces, dv=HEAD_DIM,
                        scale=HEAD_DIM ** -0.5, q_col=0, k_col=1, qk_width=BRANCH_W)

    diff = (lw['lam_q1'], lw['lam_k1'], lw['lam_q2'], lw['lam_k2'], lw['g_df_sub'], lam_init)
    df_ctx = (ctx['df_k'], ctx['df_v'], layer, 0) if latent else None
    yb = _attention(qkv, qkv, qkv, 5, heads=DF_HEADS, maps=2, pieces=_df_pieces, dv=2 * HEAD_DIM,
                    scale=HEAD_DIM ** -0.5, ctx=df_ctx, diff=diff, q_col=3, k_col=4, qk_width=BRANCH_W)

    consts = _mla_consts()
    qc, kc, vc, ckv, krope = _mla_prep(g2, lw['g_mla_cq'], lw['g_mla_ckv'], lw['w_uq'], lw['w_ukv'],
                                       lw['g_mla_q'], lw['g_mla_k'], consts, rope)
    mla_ctx = None
    if latent:
        kcx, vcx = _mla_ctx(ctx['mla_ckv'], ctx['mla_krope'], layer, lw['w_ukv'], lw['g_mla_k'], consts)
        mla_ctx = (kcx, vcx, 0, 0)
    yc = _attention(qc, kc, vc, 0, heads=MLA_HEADS, maps=1, pieces=_mla_pieces, dv=V_DIM,
                    scale=MLA_QK ** -0.5, ctx=mla_ctx)

    xs, bm, cm, dt = _ssd_prep(g3, lw['conv_w'], lw['conv_b'], lw['dt_bias'])
    y2, s_fin = _ssd_scan(xs, bm, cm, dt, lw['a_log'], ctx['ssd'] if latent else None, layer)
    yd = _ssd_post(y2, xs, g3, lw['d_skip'], lw['g_ssd'])

    ys = tuple(y.reshape(flat + (BRANCH_W,)) for y in (ya, yb, yc, yd))
    merged = _merge(h, lw['w_gate'], ys, lw['w_branch'], layer)
    if latent:
        return merged, None

    def group(n, *head_shape):
        return qkv[..., n * BRANCH_W:(n + 1) * BRANCH_W].reshape((bsz, t_len) + head_shape)

    new = dict(
        na_k=group(1, NA_HEADS, HEAD_DIM), na_v=group(2, NA_HEADS, HEAD_DIM),
        df_k=group(4, DF_HEADS, 2, HEAD_DIM), df_v=group(5, DF_HEADS, 2 * HEAD_DIM),
        mla_ckv=ckv, mla_krope=krope, ssd=s_fin)
    return merged, new


def kernel(x_prompt, x_sample, cache_na_k, cache_na_v, cache_df_k, cache_df_v, cache_mla_ckv, cache_mla_krope, state_ssd, c, c_ctx, w_mod, b_mod, g_norm1, g_norm2, w_in, g_na_q, g_na_k, rpb, g_df_q, g_df_k, lam_q1, lam_k1, lam_q2, lam_k2, g_df_sub, g_mla_cq, g_mla_ckv, w_uq, w_ukv, g_mla_q, g_mla_k, conv_w, conv_b, dt_bias, a_log, d_skip, g_ssd, w_branch, w_o, w_router, b_router, w_gu, b_gu, w_dn, b_dn):
    depth, d, _ = w_in.shape
    dec_b, _, past = cache_na_k.shape[:3]
    n_p = x_prompt.shape[0] * x_prompt.shape[1]
    n_exp = w_router.shape[-1]
    xp, xs = x_prompt.reshape(1, n_p, d), x_sample

    n_rows = -(-(dec_b + 1) // 8) * 8
    cc = jnp.zeros((n_rows, d), F32).at[:dec_b].set(c).at[dec_b].set(c_ctx)
    mods = _mod_params(cc, w_mod, b_mod)

    ctx_all = dict(
        na_k=cache_na_k.reshape(dec_b, depth, past, NA_HEADS * HEAD_DIM),
        na_v=cache_na_v.reshape(dec_b, depth, past, NA_HEADS * HEAD_DIM),
        df_k=cache_df_k.reshape(dec_b, depth, past, DF_HEADS * 2 * HEAD_DIM),
        df_v=cache_df_v.reshape(dec_b, depth, past, DF_HEADS * 2 * HEAD_DIM),
        mla_ckv=cache_mla_ckv, mla_krope=cache_mla_krope, ssd=state_ssd)

    o_lat = 6 * BRANCH_W
    o_ssd = o_lat + Q_RANK + KV_RANK + ROPE_DIM
    o_gate = o_ssd + BRANCH_W + conv_w.shape[-1] + 2 * SSD_HEADS
    ne_pad = -(-n_exp // LANE) * LANE

    names = ('na_k', 'na_v', 'df_k', 'df_v', 'mla_ckv', 'mla_krope', 'ssd')
    new = {n: [] for n in names}
    for l in range(depth):
        lam_init = 0.8 - 0.6 * math.exp(-0.3 * l)
        wq, wkv, gq, gk = _mla_perm_weights(w_uq[l], w_ukv[l], g_mla_q[l], g_mla_k[l])
        w_l = w_in[l]
        lw = dict(
            w_qkv=w_l[:, :o_lat].astype(BF16), w_lat=w_l[:, o_lat:o_ssd].astype(BF16),
            w_ssd=w_l[:, o_ssd:o_gate].astype(BF16),
            w_gate=w_l[:, o_gate:].astype(BF16),
            qkv_gains=jnp.stack([jnp.tile(g, BRANCH_W // HEAD_DIM) for g in (
                g_na_q[l], g_na_k[l], jnp.ones_like(g_na_q[l]), g_df_q[l], g_df_k[l], jnp.ones_like(g_df_q[l])
            )]).reshape(QKV_GROUPS, 1, BRANCH_W),
            rpb=rpb[l], lam_q1=lam_q1[l], lam_k1=lam_k1[l], lam_q2=lam_q2[l], lam_k2=lam_k2[l], g_df_sub=g_df_sub[l],
            g_mla_cq=g_mla_cq[l], g_mla_ckv=g_mla_ckv[l], w_uq=wq, w_ukv=wkv, g_mla_q=gq, g_mla_k=gk,
            conv_w=conv_w[l], conv_b=conv_b[l], dt_bias=dt_bias[l], a_log=a_log[l], d_skip=d_skip[l],
            g_ssd=g_ssd[l], w_branch=w_branch)
        mc = [m[:, None, :] for m in jnp.split(mods[l, :dec_b], 6, axis=-1)]
        mx = [m[:, None, :] for m in jnp.split(mods[l, dec_b:dec_b + 1], 6, axis=-1)]

        hp = _norm_mod(xp, g_norm1[l], mx[0], mx[1])
        mp, ctx_new = _token_mixers(hp, x_prompt.shape[:2], lw, l, lam_init, None)
        xp = _mm_residual(mp, w_o, l, xp, mx[2])
        hs = _norm_mod(xs, g_norm1[l], mc[0], mc[1])
        ms, _ = _token_mixers(hs, x_sample.shape[:2], lw, l, lam_init, ctx_all)
        xs = _mm_residual(ms, w_o, l, xs, mc[2])

        wr = jnp.zeros((d, ne_pad), F32).at[:, :n_exp].set(w_router[l])
        br = jnp.zeros((1, ne_pad), F32).at[0, :n_exp].set(b_router[l])
        hp2, lg_p = _norm_mod(xp, g_norm2[l], mx[3], mx[4], router=(wr, br))
        hs2, lg_s = _norm_mod(xs, g_norm2[l], mc[3], mc[4], router=(wr, br))
        h2 = jnp.concatenate([hp2.reshape(-1, d), hs2.reshape(-1, d)], axis=0)
        logits = jnp.concatenate([lg_p.reshape(-1, ne_pad), lg_s.reshape(-1, ne_pad)], axis=0)[:, :n_exp]
        y4, gate_w = _moe(h2, logits, w_gu, b_gu, w_dn, b_dn, l)
        xp = _moe_combine(xp, mx[5], y4, gate_w, 0)
        xs = _moe_combine(xs, mc[5], y4, gate_w, n_p)
        for n in names:
            new[n].append(ctx_new[n])
    return (xp.reshape(x_prompt.shape), xs) + tuple(jnp.stack(new[n], axis=1) for n in names)
```

```python
import functools
import math

import jax
import jax.numpy as jnp
from jax import lax
import numpy as np
from jax.experimental import pallas as pl
from jax.experimental.pallas import tpu as pltpu

F32 = jnp.float32
BF16 = jnp.bfloat16

GRID_W = 64
HEAD_DIM = 64
ROPE_BASE = 10000.0
EPS = 1e-6
NA_HEADS = 8
WIN_R = 8
WIN_C = 16
DF_HEADS = 4
MLA_HEADS = 4
Q_RANK = 384
KV_RANK = 128
NOPE_DIM = 128
ROPE_DIM = 64
V_DIM = 128
MLA_QK = NOPE_DIM + ROPE_DIM
SSD_HEADS = 8
SSD_HEAD_DIM = 64
SSD_GROUPS = 2
D_STATE = 128
CHUNK = 128
TOP_K = 4
SWIGLU_ALPHA = 1.702
SWIGLU_LIMIT = 7.0
N_BRANCH = 4
BRANCH_W = 512

V7X_VMEM_BYTES = 64 * 1024 * 1024
VMEM_LIMIT = V7X_VMEM_BYTES - 8 * 1024 * 1024
MOE_VMEM_LIMIT = V7X_VMEM_BYTES - 2 * 1024 * 1024
LANE = 128

NA_ROWS_PER_STEP = 4
MOE_TOKENS = 512
MOE_ITEM_SUBS = 2
MOE_FF_CHUNK = 512


def _cparams(*sem, vmem_limit=VMEM_LIMIT):
    return pltpu.CompilerParams(dimension_semantics=sem, vmem_limit_bytes=vmem_limit)


def _tile(n, pref):
    t = min(n, pref)
    while n % t:
        t //= 2
    return t


def _dot(a, b):
    return jnp.dot(a, b, preferred_element_type=F32)


def _dot_nt(a, b):
    return lax.dot_general(a, b, (((1,), (1,)), ((), ())), preferred_element_type=F32)


def _dot_tn(a, b):
    return lax.dot_general(a, b, (((0,), (0,)), ((), ())), preferred_element_type=F32)


def _split3(x):
    h = x.astype(BF16)
    r = x - h.astype(F32)
    m = r.astype(BF16)
    lo = (r - m.astype(F32)).astype(BF16)
    return h, m, lo


def _sel_dot(x, sel):
    h, m, lo = _split3(x)
    return _dot(h, sel) + _dot(m, sel) + _dot(lo, sel)


def _dot_hi(a, b):
    a1, a2, _ = _split3(a)
    b1, b2, _ = _split3(b)
    return _dot(a1, b1) + (_dot(a1, b2) + _dot(a2, b1))


def _rope_rotate(y, cos, sin):
    w = y.shape[-1]
    lane = lax.broadcasted_iota(jnp.int32, y.shape, 1)
    partner = jnp.where((lane % 32) < 16, pltpu.roll(y, w - 16, 1), pltpu.roll(y, 16, 1))
    return y * cos + partner * sin


def _rope_tables(t_len, width):
    half = HEAD_DIM // 2
    inv = jnp.asarray(ROPE_BASE ** (-np.arange(0, half, 2) / half), F32)
    t = np.arange(t_len)
    cols = []
    sins = []
    for pos in (t // GRID_W, t % GRID_W):
        ang = jnp.asarray(pos, F32)[:, None] * inv[None, :]
        c, s = jnp.cos(ang), jnp.sin(ang)
        cols += [c, c]
        sins += [-s, s]
    cos = jnp.concatenate(cols, axis=-1)
    sin = jnp.concatenate(sins, axis=-1)
    rep = width // HEAD_DIM
    return jnp.tile(cos, (1, rep)), jnp.tile(sin, (1, rep))


def _group_selector(width, group):
    g = np.arange(width) // group
    return jnp.asarray(g[:, None] == g[None, :], BF16)


def _mod_kernel(c_ref, w_ref, b_ref, o_ref):
    c = c_ref[...]
    s = (c * jax.nn.sigmoid(c)).astype(BF16)
    o_ref[0] = _dot(s, w_ref[0].astype(BF16)) + b_ref[0]


def _mod_params(cc, w_mod, b_mod):
    depth, d, n = w_mod.shape
    rows = cc.shape[0]
    tn = _tile(n, 1024)
    return pl.pallas_call(
        _mod_kernel,
        grid=(depth, n // tn),
        in_specs=[pl.BlockSpec((rows, d), lambda l, j: (0, 0)),
                  pl.BlockSpec((1, d, tn), lambda l, j: (l, 0, j)),
                  pl.BlockSpec((1, 1, tn), lambda l, j: (l, 0, j))],
        out_specs=pl.BlockSpec((1, rows, tn), lambda l, j: (l, 0, j)),
        out_shape=jax.ShapeDtypeStruct((depth, rows, n), F32),
        compiler_params=_cparams("parallel", "parallel"),
    )(cc, w_mod, b_mod.reshape(depth, 1, n))


def _norm_mod_body(x_ref, g_ref, sh_ref, sc_ref):
    x = x_ref[0]
    y = x * lax.rsqrt(jnp.mean(x * x, axis=-1, keepdims=True) + EPS)
    y = y * g_ref[...]
    return y * (1.0 + sc_ref[0]) + sh_ref[0]


def _norm_mod_kernel(x_ref, g_ref, sh_ref, sc_ref, o_ref):
    o_ref[0] = _norm_mod_body(x_ref, g_ref, sh_ref, sc_ref).astype(o_ref.dtype)


def _norm_mod_router_kernel(x_ref, g_ref, sh_ref, sc_ref, wr_ref, br_ref, o_ref, lg_ref):
    y = _norm_mod_body(x_ref, g_ref, sh_ref, sc_ref)
    o_ref[0] = y.astype(o_ref.dtype)
    lg_ref[0] = _dot_hi(y, wr_ref[...]) + br_ref[...]


def _mod_spec(m, d):
    if m.shape[0] == 1:
        return pl.BlockSpec((1, 1, d), lambda b, t: (0, 0, 0))
    return pl.BlockSpec((1, 1, d), lambda b, t: (b, 0, 0))


def _norm_mod(x, g, shift, scale, router=None):
    bsz, t_len, d = x.shape
    tt = _tile(t_len, 256)
    in_specs = [pl.BlockSpec((1, tt, d), lambda b, t: (b, t, 0)),
                pl.BlockSpec((1, d), lambda b, t: (0, 0)),
                _mod_spec(shift, d), _mod_spec(scale, d)]
    out_spec = pl.BlockSpec((1, tt, d), lambda b, t: (b, t, 0))
    out_shape = jax.ShapeDtypeStruct((bsz, t_len, d), BF16 if router is None else F32)
    args = [x, g.reshape(1, d), shift, scale]
    if router is None:
        return pl.pallas_call(
            _norm_mod_kernel, grid=(bsz, t_len // tt), in_specs=in_specs, out_specs=out_spec,
            out_shape=out_shape, compiler_params=_cparams("parallel", "parallel"))(*args)
    wr, br = router
    ne = wr.shape[1]
    in_specs += [pl.BlockSpec((d, ne), lambda b, t: (0, 0)), pl.BlockSpec((1, ne), lambda b, t: (0, 0))]
    return pl.pallas_call(
        _norm_mod_router_kernel, grid=(bsz, t_len // tt), in_specs=in_specs,
        out_specs=[out_spec, pl.BlockSpec((1, tt, ne), lambda b, t: (b, t, 0))],
        out_shape=[out_shape, jax.ShapeDtypeStruct((bsz, t_len, ne), F32)],
        compiler_params=_cparams("parallel", "parallel"))(*args, wr, br)


def _mm_kernel(x_ref, w_ref, o_ref):
    o_ref[0] = _dot(x_ref[0].astype(BF16), w_ref[...].astype(BF16)).astype(o_ref.dtype)


def _mm_res_kernel(x_ref, w_ref, r_ref, g_ref, o_ref):
    o_ref[0] = r_ref[0] + g_ref[0] * _dot(x_ref[0].astype(BF16), w_ref[...].astype(BF16))


def _w_spec(w, layer, k, tn):
    if w.ndim == 3:
        return pl.BlockSpec((None, k, tn), lambda n, b, t: (layer, 0, n))
    return pl.BlockSpec((k, tn), lambda n, b, t: (0, n))


def _mm(x, w, layer=0, out_dtype=F32, tm_pref=1024, tn_pref=512):
    bsz, t_len, k = x.shape
    n = w.shape[-1]
    tm = _tile(t_len, tm_pref)
    tn = n if n % LANE else _tile(n, tn_pref)
    return pl.pallas_call(
        _mm_kernel,
        grid=(n // tn, bsz, t_len // tm),
        in_specs=[pl.BlockSpec((1, tm, k), lambda n_, b, t: (b, t, 0)), _w_spec(w, layer, k, tn)],
        out_specs=pl.BlockSpec((1, tm, tn), lambda n_, b, t: (b, t, n_)),
        out_shape=jax.ShapeDtypeStruct((bsz, t_len, n), out_dtype),
        compiler_params=_cparams("parallel", "parallel", "parallel"),
    )(x, w)


def _mm_residual(x, w, layer, res, gate):
    bsz, t_len, k = x.shape
    n = w.shape[-1]
    tm = _tile(t_len, 1024)
    tn = _tile(n, 512)
    if gate.shape[0] == 1:
        g_spec = pl.BlockSpec((1, 1, tn), lambda n_, b, t: (0, 0, n_))
    else:
        g_spec = pl.BlockSpec((1, 1, tn), lambda n_, b, t: (b, 0, n_))
    return pl.pallas_call(
        _mm_res_kernel,
        grid=(n // tn, bsz, t_len // tm),
        in_specs=[pl.BlockSpec((1, tm, k), lambda n_, b, t: (b, t, 0)), _w_spec(w, layer, k, tn),
                  pl.BlockSpec((1, tm, tn), lambda n_, b, t: (b, t, n_)), g_spec],
        out_specs=pl.BlockSpec((1, tm, tn), lambda n_, b, t: (b, t, n_)),
        out_shape=jax.ShapeDtypeStruct((bsz, t_len, n), F32),
        compiler_params=_cparams("parallel", "parallel", "parallel"),
    )(x, w, res, gate)


QKV_GROUPS = 6
QKV_VALUE_GROUPS = (2, 5)
QKV_ROPE_GROUPS = (3, 4)


def _qkv_kernel(*refs, rope):
    if rope:
        x_ref, w_ref, g_ref, e_ref, cos_ref, sin_ref, o_ref = refs
    else:
        x_ref, w_ref, g_ref, e_ref, o_ref = refs
    n = pl.program_id(0)
    y = _dot(x_ref[0], w_ref[...])
    is_value = functools.reduce(jnp.logical_or, [n == v for v in QKV_VALUE_GROUPS])

    @pl.when(is_value)
    def _():
        o_ref[0] = y

    @pl.when(jnp.logical_not(is_value))
    def _():
        ms = _sel_dot(y * y, e_ref[...]) * (1.0 / HEAD_DIM)
        yn = y * lax.rsqrt(ms + EPS) * g_ref[0]
        if rope:
            is_rope = functools.reduce(jnp.logical_or, [n == v for v in QKV_ROPE_GROUPS])

            @pl.when(is_rope)
            def _():
                o_ref[0] = _rope_rotate(yn, cos_ref[...], sin_ref[...])

            @pl.when(jnp.logical_not(is_rope))
            def _():
                o_ref[0] = yn
        else:
            o_ref[0] = yn


def _qkv_proj(h, w_qkv, gains, rope=None):
    bsz, t_len, k = h.shape
    w = BRANCH_W
    tm = _tile(t_len, 1024)
    in_specs = [pl.BlockSpec((1, tm, k), lambda n, b, t: (b, t, 0)),
                pl.BlockSpec((k, w), lambda n, b, t: (0, n)),
                pl.BlockSpec((1, 1, w), lambda n, b, t: (n, 0, 0)),
                pl.BlockSpec((w, w), lambda n, b, t: (0, 0))]
    args = [h, w_qkv, gains, _group_selector(w, HEAD_DIM)]
    if rope is not None:
        in_specs += [pl.BlockSpec((tm, w), lambda n, b, t: (t, 0))] * 2
        args += list(rope)
    return pl.pallas_call(
        functools.partial(_qkv_kernel, rope=rope is not None),
        grid=(QKV_GROUPS, bsz, t_len // tm), in_specs=in_specs,
        out_specs=pl.BlockSpec((1, tm, w), lambda n, b, t: (b, t, n)),
        out_shape=jax.ShapeDtypeStruct((bsz, t_len, QKV_GROUPS * w), F32),
        compiler_params=_cparams("parallel", "parallel", "parallel"),
    )(*args)


def _attn_kernel(*refs, heads, maps, pieces, dv, scale, has_ctx, diff, post_scale):
    it = iter(refs)
    q_ref, k_ref, v_ref = next(it), next(it), next(it)
    kc_ref = vc_ref = None
    if has_ctx:
        kc_ref, vc_ref = next(it), next(it)
    if diff is not None:
        lq1, lk1, lq2, lk2, gsub_ref = next(it), next(it), next(it), next(it), next(it)
    o_ref = next(it)

    q = q_ref[0].astype(BF16)
    k = k_ref[0].astype(BF16)
    v = v_ref[0].astype(BF16)
    if has_ctx:
        kc = kc_ref[0, 0].astype(BF16)
        vc = vc_ref[0, 0].astype(BF16)
    if diff is not None:
        lam = (jnp.exp(jnp.sum(lq1[...] * lk1[...], axis=-1, keepdims=True))
               - jnp.exp(jnp.sum(lq2[...] * lk2[...], axis=-1, keepdims=True)) + diff)

    for h in range(heads):
        w_own = w_ctx = None
        for m in range(maps):
            s = s_c = None
            for (qo, ko, d) in pieces(h, m):
                part = _dot_nt(q[:, qo:qo + d], k[:, ko:ko + d])
                s = part if s is None else s + part
                if has_ctx:
                    part_c = _dot_nt(q[:, qo:qo + d], kc[:, ko:ko + d])
                    s_c = part_c if s_c is None else s_c + part_c
            s = s * scale
            mx = jnp.max(s, axis=-1, keepdims=True)
            if has_ctx:
                s_c = s_c * scale
                mx = jnp.maximum(mx, jnp.max(s_c, axis=-1, keepdims=True))
            p = jnp.exp(s - mx)
            den = jnp.sum(p, axis=-1, keepdims=True)
            if has_ctx:
                p_c = jnp.exp(s_c - mx)
                den = den + jnp.sum(p_c, axis=-1, keepdims=True)
            inv = 1.0 / den
            if m == 1:
                inv = -lam * inv
            w_own = p * inv if w_own is None else w_own + p * inv
            if has_ctx:
                w_ctx = p_c * inv if w_ctx is None else w_ctx + p_c * inv
        o = _dot(w_own.astype(BF16), v[:, h * dv:(h + 1) * dv])
        if has_ctx:
            o = o + _dot(w_ctx.astype(BF16), vc[:, h * dv:(h + 1) * dv])
        if diff is not None:
            o = o * lax.rsqrt(jnp.mean(o * o, axis=-1, keepdims=True) + EPS) * gsub_ref[...] * post_scale
        o_ref[0, :, h * dv:(h + 1) * dv] = o.astype(o_ref.dtype)


def _attention(q, k, v, v_col, *, heads, maps, pieces, dv, scale, ctx=None, diff=None, q_col=0, k_col=0,
               qk_width=None):
    bsz, t_len, _ = q.shape
    wq = wk = qk_width if qk_width is not None else q.shape[-1]
    wv = heads * dv
    tq = _tile(t_len, 256)
    in_specs = [pl.BlockSpec((1, tq, wq), lambda b, t: (b, t, q_col)),
                pl.BlockSpec((1, t_len, wk), lambda b, t: (b, 0, k_col)),
                pl.BlockSpec((1, t_len, wv), lambda b, t: (b, 0, v_col))]
    args = [q, k, v]
    if ctx is not None:
        kc, vc, layer, vc_col = ctx
        tc = kc.shape[2]
        in_specs += [pl.BlockSpec((1, 1, tc, wk), lambda b, t: (b, layer, 0, 0)),
                     pl.BlockSpec((1, 1, tc, wv), lambda b, t: (b, layer, 0, vc_col))]
        args += [kc, vc]
    lam_init = 0.0
    if diff is not None:
        lq1, lk1, lq2, lk2, gsub, lam_init = diff
        for a in (lq1, lk1, lq2, lk2, gsub):
            in_specs.append(pl.BlockSpec((1, a.shape[0]), lambda b, t: (0, 0)))
            args.append(a.reshape(1, -1))
    kern = functools.partial(
        _attn_kernel, heads=heads, maps=maps, pieces=pieces, dv=dv, scale=scale,
        has_ctx=ctx is not None, diff=lam_init if diff is not None else None,
        post_scale=1.0 - lam_init)
    return pl.pallas_call(
        kern, grid=(bsz, t_len // tq), in_specs=in_specs,
        out_specs=pl.BlockSpec((1, tq, wv), lambda b, t: (b, t, 0)),
        out_shape=jax.ShapeDtypeStruct((bsz, t_len, wv), BF16),
        compiler_params=_cparams("parallel", "parallel"),
    )(*args)


def _na_pieces(h, m):
    return [(h * HEAD_DIM, h * HEAD_DIM, HEAD_DIM)]


def _df_pieces(h, m):
    o = (2 * h + m) * HEAD_DIM
    return [(o, o, HEAD_DIM)]


def _mla_pieces(h, m):
    nope_w = MLA_HEADS * NOPE_DIM
    return [(h * NOPE_DIM, h * NOPE_DIM, NOPE_DIM),
            (nope_w + h * ROPE_DIM, nope_w + h * ROPE_DIM, ROPE_DIM)]


def _na_latent_kernel(q_ref, k_ref, v_ref, kc_ref, vc_ref, bias_ref, o_ref, *, rows, kr, rps, scale):
    kc = kc_ref[0, 0].astype(BF16)
    vc = vc_ref[0, 0].astype(BF16)
    for rr in range(rps):
        r = pl.program_id(1) * rps + rr
        start_row = jnp.clip(r - WIN_R // 2, 0, rows - kr)
        start = pl.multiple_of(start_row * GRID_W, GRID_W)
        q = q_ref[0, rr * GRID_W:(rr + 1) * GRID_W, :].astype(BF16)
        kl = k_ref[0, pl.ds(start, kr * GRID_W), :].astype(BF16)
        vl = v_ref[0, pl.ds(start, kr * GRID_W), :].astype(BF16)
        ro0 = start_row - r + WIN_R - 1
        s_heads = []
        for h in range(NA_HEADS):
            sl = slice(h * HEAD_DIM, (h + 1) * HEAD_DIM)
            bias = jnp.concatenate([bias_ref[h, ro0 + 2 * j] for j in range(kr // 2)], axis=-1)
            s_loc = _dot_nt(q[:, sl], kl[:, sl]) * scale + bias
            s_ctx = _dot_nt(q[:, sl], kc[:, sl]) * scale
            s_heads.append(jnp.concatenate([s_loc, s_ctx], axis=-1))
        s = jnp.concatenate(s_heads, axis=0)
        p = jnp.exp(s - jnp.max(s, axis=-1, keepdims=True))
        pn = (p * (1.0 / jnp.sum(p, axis=-1, keepdims=True))).astype(BF16)
        n_loc = kr * GRID_W
        for h in range(NA_HEADS):
            sl = slice(h * HEAD_DIM, (h + 1) * HEAD_DIM)
            ph = pn[h * GRID_W:(h + 1) * GRID_W]
            o = _dot(ph[:, :n_loc], vl[:, sl]) + _dot(ph[:, n_loc:], vc[:, sl])
            o_ref[0, rr * GRID_W:(rr + 1) * GRID_W, sl] = o.astype(o_ref.dtype)


def _na_bias_table(rpb_l):
    qc = np.arange(GRID_W)
    kc = np.arange(GRID_W)
    qstart = np.clip(qc - WIN_C // 2, 0, GRID_W - WIN_C)
    valid = (kc[None, :] >= qstart[:, None]) & (kc[None, :] < qstart[:, None] + WIN_C)
    c_off = np.clip(kc[None, :] - qc[:, None] + WIN_C - 1, 0, 2 * WIN_C - 2)
    onehot = (c_off[None] == np.arange(2 * WIN_C - 1)[:, None, None]) & valid[None]
    tab = jnp.einsum('hrc,cqk->hrqk', rpb_l.astype(F32), jnp.asarray(onehot, F32),
                     precision=lax.Precision.HIGHEST)
    tab = jnp.where(valid[None, None], tab, -jnp.inf)
    return jnp.concatenate([tab[:, :-1], tab[:, 1:]], axis=-1)


def _na_latent(q, k, v, v_col, kc, vc, layer, rpb_l, q_col=0, k_col=0):
    bsz, t_len, _ = q.shape
    w = NA_HEADS * HEAD_DIM
    rows = t_len // GRID_W
    kr = min(WIN_R, rows)
    assert kr % 2 == 0, "the bias table pairs key rows"
    tc = kc.shape[2]
    bias = _na_bias_table(rpb_l)
    rps = _tile(rows, NA_ROWS_PER_STEP)
    return pl.pallas_call(
        functools.partial(_na_latent_kernel, rows=rows, kr=kr, rps=rps, scale=HEAD_DIM ** -0.5),
        grid=(bsz, rows // rps),
        in_specs=[pl.BlockSpec((1, rps * GRID_W, w), lambda b, r: (b, r, q_col)),
                  pl.BlockSpec((1, t_len, w), lambda b, r: (b, 0, k_col)),
                  pl.BlockSpec((1, t_len, w), lambda b, r: (b, 0, v_col)),
                  pl.BlockSpec((1, 1, tc, w), lambda b, r: (b, layer, 0, 0)),
                  pl.BlockSpec((1, 1, tc, w), lambda b, r: (b, layer, 0, 0)),
                  _const_spec(bias)],
        out_specs=pl.BlockSpec((1, rps * GRID_W, w), lambda b, r: (b, r, 0)),
        out_shape=jax.ShapeDtypeStruct((bsz, t_len, w), BF16),
        compiler_params=_cparams("parallel", "parallel"),
    )(q, k, v, kc, vc, bias)


def _mla_k_part(ckv_bf16, krope, wukv_ref, gk_ref, ek_ref, tile_ref, rope_tabs):
    nope_w = MLA_HEADS * NOPE_DIM
    kv = _dot(ckv_bf16, wukv_ref[...])
    k_nope = kv[:, :nope_w]
    kr_t = _sel_dot(krope, tile_ref[...])
    sq = jnp.concatenate([k_nope * k_nope, krope * krope], axis=-1)
    ms = _sel_dot(sq, ek_ref[...]) * (1.0 / MLA_QK)
    rs = lax.rsqrt(ms + EPS)
    g = gk_ref[...]
    kn = k_nope * rs[:, :nope_w] * g[:, :nope_w]
    kro = kr_t * rs[:, nope_w:] * g[:, nope_w:]
    if rope_tabs is not None:
        kro = _rope_rotate(kro, rope_tabs[0][...], rope_tabs[1][...])
    return jnp.concatenate([kn, kro], axis=-1), kv[:, nope_w:]


def _mla_prep_kernel(*refs, rope):
    it = iter(refs)
    x_ref = next(it)
    gcq_ref, gckv_ref, wuq_ref, wukv_ref, gq_ref, gk_ref, eq_ref, ek_ref, tile_ref = (next(it) for _ in range(9))
    tabs = (next(it), next(it)) if rope else None
    q_ref, k_ref, v_ref, ckv_ref, kr_ref = (next(it) for _ in range(5))
    nope_w = MLA_HEADS * NOPE_DIM

    x = x_ref[0]
    dq = x[:, :Q_RANK]
    cq = dq * lax.rsqrt(jnp.mean(dq * dq, axis=-1, keepdims=True) + EPS) * gcq_ref[...]
    dkv = x[:, Q_RANK:Q_RANK + KV_RANK]
    ckv = dkv * lax.rsqrt(jnp.mean(dkv * dkv, axis=-1, keepdims=True) + EPS) * gckv_ref[...]
    krope = x[:, Q_RANK + KV_RANK:]
    ckv_ref[0] = ckv
    kr_ref[0] = krope

    qr = _dot(cq.astype(BF16), wuq_ref[...])
    ms = _sel_dot(qr * qr, eq_ref[...]) * (1.0 / MLA_QK)
    qn = qr * lax.rsqrt(ms + EPS) * gq_ref[...]
    if rope:
        q_ref[0] = jnp.concatenate(
            [qn[:, :nope_w], _rope_rotate(qn[:, nope_w:], tabs[0][...], tabs[1][...])], axis=-1)
    else:
        q_ref[0] = qn
    kn, v = _mla_k_part(ckv.astype(BF16), krope, wukv_ref, gk_ref, ek_ref, tile_ref, tabs)
    k_ref[0] = kn
    v_ref[0] = v.astype(v_ref.dtype)


def _mla_ctx_kernel(ckv_ref, kr_ref, wukv_ref, gk_ref, ek_ref, tile_ref, k_ref, v_ref):
    kn, v = _mla_k_part(ckv_ref[0, 0].astype(BF16), kr_ref[0, 0], wukv_ref, gk_ref, ek_ref, tile_ref, None)
    k_ref[0, 0] = kn
    v_ref[0, 0] = v.astype(v_ref.dtype)


def _mla_consts():
    nope_w = MLA_HEADS * NOPE_DIM
    qk_w = nope_w + MLA_HEADS * ROPE_DIM
    head_q = np.concatenate([np.arange(nope_w) // NOPE_DIM, np.arange(MLA_HEADS * ROPE_DIM) // ROPE_DIM])
    eq = head_q[:, None] == head_q[None, :]
    ek = np.concatenate([eq[:nope_w], np.ones((ROPE_DIM, qk_w), bool)], axis=0)
    tile = np.arange(ROPE_DIM)[:, None] == (np.arange(MLA_HEADS * ROPE_DIM) % ROPE_DIM)[None, :]
    return jnp.asarray(eq, BF16), jnp.asarray(ek, BF16), jnp.asarray(tile, BF16)


def _mla_perm_weights(w_uq_l, w_ukv_l, g_q, g_k):
    wq = w_uq_l.reshape(Q_RANK, MLA_HEADS, MLA_QK)
    wq = jnp.concatenate([wq[:, :, :NOPE_DIM].reshape(Q_RANK, -1), wq[:, :, NOPE_DIM:].reshape(Q_RANK, -1)], axis=1)
    wkv = w_ukv_l.reshape(KV_RANK, MLA_HEADS, NOPE_DIM + V_DIM)
    wkv = jnp.concatenate([wkv[:, :, :NOPE_DIM].reshape(KV_RANK, -1), wkv[:, :, NOPE_DIM:].reshape(KV_RANK, -1)], axis=1)

    def gain(g):
        return jnp.concatenate([jnp.tile(g[:NOPE_DIM], MLA_HEADS), jnp.tile(g[NOPE_DIM:], MLA_HEADS)]).reshape(1, -1)

    return wq.astype(BF16), wkv.astype(BF16), gain(g_q), gain(g_k)


def _const_spec(a):
    nd = a.ndim
    return pl.BlockSpec(a.shape, lambda *_: (0,) * nd)


def _mla_prep(x, g_cq, g_ckv, wq, wkv, gq, gk, consts, rope):
    bsz, t_len, w = x.shape
    tt = _tile(t_len, 256)
    eq, ek, tile = consts
    qk_w = MLA_HEADS * MLA_QK
    args = [x, g_cq.reshape(1, -1), g_ckv.reshape(1, -1), wq, wkv, gq, gk, eq, ek, tile]
    in_specs = [pl.BlockSpec((1, tt, w), lambda b, t: (b, t, 0))] + [_const_spec(a) for a in args[1:]]
    if rope is not None:
        rw = MLA_HEADS * ROPE_DIM
        in_specs += [pl.BlockSpec((tt, rw), lambda b, t: (t, 0))] * 2
        args += [rope[0][:, :rw], rope[1][:, :rw]]
    widths = (qk_w, qk_w, MLA_HEADS * V_DIM, KV_RANK, ROPE_DIM)
    dtypes = (F32, F32, BF16, F32, F32)
    return pl.pallas_call(
        functools.partial(_mla_prep_kernel, rope=rope is not None),
        grid=(bsz, t_len // tt), in_specs=in_specs,
        out_specs=[pl.BlockSpec((1, tt, wd), lambda b, t: (b, t, 0)) for wd in widths],
        out_shape=[jax.ShapeDtypeStruct((bsz, t_len, wd), dt) for wd, dt in zip(widths, dtypes)],
        compiler_params=_cparams("parallel", "parallel"),
    )(*args)


def _mla_ctx(cache_ckv, cache_krope, layer, wkv, gk, consts):
    bsz, depth, tc, _ = cache_ckv.shape
    _, ek, tile = consts
    qk_w = MLA_HEADS * MLA_QK
    vw = MLA_HEADS * V_DIM
    args = [cache_ckv, cache_krope, wkv, gk, ek, tile]
    in_specs = [pl.BlockSpec((1, 1, tc, KV_RANK), lambda b: (b, layer, 0, 0)),
                pl.BlockSpec((1, 1, tc, ROPE_DIM), lambda b: (b, layer, 0, 0))]
    in_specs += [_const_spec(a) for a in args[2:]]
    return pl.pallas_call(
        _mla_ctx_kernel, grid=(bsz,), in_specs=in_specs,
        out_specs=[pl.BlockSpec((1, 1, tc, qk_w), lambda b: (b, 0, 0, 0)),
                   pl.BlockSpec((1, 1, tc, vw), lambda b: (b, 0, 0, 0))],
        out_shape=[jax.ShapeDtypeStruct((bsz, 1, tc, qk_w), F32), jax.ShapeDtypeStruct((bsz, 1, tc, vw), BF16)],
        compiler_params=_cparams("parallel"),
    )(*args)


def _ssd_prep_kernel(x_ref, cw_ref, cb_ref, dtb_ref, xs_ref, bm_ref, cm_ref, dt_ref):
    t_len = x_ref.shape[1]
    cw_total = cw_ref.shape[1]
    xbc = x_ref[0, :, BRANCH_W:BRANCH_W + cw_total]
    row = lax.broadcasted_iota(jnp.int32, xbc.shape, 0)
    prev1 = jnp.where(row >= 1, pltpu.roll(xbc, 1, 0), 0.0)
    next1 = jnp.where(row < t_len - 1, pltpu.roll(xbc, t_len - 1, 0), 0.0)
    next2 = jnp.where(row < t_len - 2, pltpu.roll(xbc, t_len - 2, 0), 0.0)
    cw = cw_ref[...]
    u = prev1 * cw[0:1] + xbc * cw[1:2] + next1 * cw[2:3] + next2 * cw[3:4] + cb_ref[...]
    u = u * jax.nn.sigmoid(u)
    xw = SSD_HEADS * SSD_HEAD_DIM
    bw = SSD_GROUPS * D_STATE
    xs_ref[0] = u[:, :xw]
    bm_ref[0] = u[:, xw:xw + bw]
    cm_ref[0] = u[:, xw + bw:]
    raw = x_ref[0, :, BRANCH_W + cw_total:] + dtb_ref[...]
    dt_ref[0] = jnp.maximum(raw, 0.0) + jnp.log1p(jnp.exp(-jnp.abs(raw)))


def _ssd_prep(g3, conv_w, conv_b, dt_bias):
    bsz, t_len, w = g3.shape
    cw = conv_w.shape[1]
    xw = SSD_HEADS * SSD_HEAD_DIM
    bw = SSD_GROUPS * D_STATE
    nh2 = 2 * SSD_HEADS
    widths = (xw, bw, bw, nh2)
    return pl.pallas_call(
        _ssd_prep_kernel, grid=(bsz,),
        in_specs=[pl.BlockSpec((1, t_len, w), lambda b: (b, 0, 0)),
                  pl.BlockSpec(conv_w.shape, lambda b: (0, 0)),
                  pl.BlockSpec((1, cw), lambda b: (0, 0)),
                  pl.BlockSpec((1, nh2), lambda b: (0, 0))],
        out_specs=[pl.BlockSpec((1, t_len, wd), lambda b: (b, 0, 0)) for wd in widths],
        out_shape=[jax.ShapeDtypeStruct((bsz, t_len, wd), F32) for wd in widths],
        compiler_params=_cparams("parallel"),
    )(g3, conv_w, conv_b.reshape(1, cw), dt_bias.reshape(1, nh2))


def _ssd_scan_kernel(*refs, has_s0, nc):
    if has_s0:
        xs_ref, bm_ref, cm_ref, dt_ref, dtt_ref, alogt_ref, s0_ref, y_ref, sfin_ref, st_ref = refs
    else:
        xs_ref, bm_ref, cm_ref, dt_ref, dtt_ref, alogt_ref, y_ref, sfin_ref, st_ref = refs
    d = pl.program_id(1)
    c = pl.program_id(2)
    chunk = xs_ref.shape[1]

    @pl.when(c == 0)
    def _():
        if has_s0:
            st_ref[...] = s0_ref[0, 0, 0]
        else:
            st_ref[...] = jnp.zeros_like(st_ref)

    a_col = -jnp.exp(alogt_ref[0])
    dt = dt_ref[0, 0]
    dta_t = dtt_ref[0, 0] * a_col
    ri = lax.broadcasted_iota(jnp.int32, (chunk, chunk), 0)
    ci = lax.broadcasted_iota(jnp.int32, (chunk, chunk), 1)
    sign = 1 - 2 * d
    mask = (ri - ci) * sign >= 0
    mask_t = jnp.where((ci - ri) * sign >= 0, 1.0, 0.0).astype(BF16)
    acs_rows = _sel_dot(dta_t, mask_t)

    xs = xs_ref[0]
    rep = SSD_HEADS // SSD_GROUPS
    cb = []
    for g in range(SSD_GROUPS):
        cg = cm_ref[0, :, g * D_STATE:(g + 1) * D_STATE].astype(BF16)
        bg = bm_ref[0, :, g * D_STATE:(g + 1) * D_STATE].astype(BF16)
        cb.append((cg, bg, _dot_nt(cg, bg)))

    heads = range(SSD_HEADS)
    dta_rows = [dta_t[h:h + 1, :] for h in heads]
    acs_cols = [jnp.sum(jnp.where(mask, dta_rows[h], 0.0), axis=-1, keepdims=True) for h in heads]
    tots = [jnp.sum(dta_rows[h], axis=-1, keepdims=True) for h in heads]
    decays = [jnp.exp(jnp.where(mask, acs_cols[h] - acs_rows[h:h + 1, :], -jnp.inf)) for h in heads]
    xdts = [xs[:, h * SSD_HEAD_DIM:(h + 1) * SSD_HEAD_DIM] * dt[:, h:h + 1] for h in heads]
    states = [st_ref[h] for h in heads]
    y_diag = [_dot((cb[h // rep][2] * decays[h]).astype(BF16), xdts[h].astype(BF16)) for h in heads]
    y_off = [_dot_nt(cb[h // rep][0], states[h].astype(BF16)) for h in heads]
    upd = [_dot_tn((xdts[h] * jnp.exp(tots[h] - acs_cols[h])).astype(BF16), cb[h // rep][1]) for h in heads]
    for h in heads:
        st_ref[h] = states[h] * jnp.exp(tots[h]) + upd[h]
        y_ref[0, 0, :, h * SSD_HEAD_DIM:(h + 1) * SSD_HEAD_DIM] = y_diag[h] + jnp.exp(acs_cols[h]) * y_off[h]

    @pl.when(c == nc - 1)
    def _():
        sfin_ref[0, 0] = st_ref[...]


def _ssd_scan(xs, bm, cm, dt, a_log_l, s0=None, layer=0):
    bsz, t_len, xw = xs.shape
    chunk = CHUNK if t_len % CHUNK == 0 else t_len
    nc = t_len // chunk
    nh = SSD_HEADS
    bw = bm.shape[-1]
    dt4 = dt.reshape(bsz, t_len, 2, nh).transpose(0, 2, 1, 3)
    dtt = dt4.transpose(0, 1, 3, 2)

    def tok(b, d, c):
        return c + d * (nc - 1 - 2 * c)

    in_specs = [pl.BlockSpec((1, chunk, xw), lambda b, d, c: (b, tok(b, d, c), 0)),
                pl.BlockSpec((1, chunk, bw), lambda b, d, c: (b, tok(b, d, c), 0)),
                pl.BlockSpec((1, chunk, bw), lambda b, d, c: (b, tok(b, d, c), 0)),
                pl.BlockSpec((1, 1, chunk, nh), lambda b, d, c: (b, d, tok(b, d, c), 0)),
                pl.BlockSpec((1, 1, nh, chunk), lambda b, d, c: (b, d, 0, tok(b, d, c))),
                pl.BlockSpec((1, nh, 1), lambda b, d, c: (d, 0, 0))]
    args = [xs, bm, cm, dt4, dtt, a_log_l.reshape(2, nh, 1)]
    if s0 is not None:
        in_specs.append(pl.BlockSpec((1, 1, 1, nh, SSD_HEAD_DIM, D_STATE), lambda b, d, c: (b, layer, d, 0, 0, 0)))
        args.append(s0)
    return pl.pallas_call(
        functools.partial(_ssd_scan_kernel, has_s0=s0 is not None, nc=nc),
        grid=(bsz, 2, nc), in_specs=in_specs,
        out_specs=[pl.BlockSpec((1, 1, chunk, xw), lambda b, d, c: (d, b, tok(b, d, c), 0)),
                   pl.BlockSpec((1, 1, nh, SSD_HEAD_DIM, D_STATE), lambda b, d, c: (b, d, 0, 0, 0))],
        out_shape=[jax.ShapeDtypeStruct((2, bsz, t_len, xw), F32),
                   jax.ShapeDtypeStruct((bsz, 2, nh, SSD_HEAD_DIM, D_STATE), F32)],
        scratch_shapes=[pltpu.VMEM((nh, SSD_HEAD_DIM, D_STATE), F32)],
        compiler_params=_cparams("parallel", "arbitrary", "arbitrary"),
    )(*args)


def _ssd_post_kernel(yf_ref, yb_ref, xs_ref, z_ref, dsk_ref, g_ref, o_ref):
    z = z_ref[0]
    y = yf_ref[0, 0] + yb_ref[0, 0] + dsk_ref[...] * xs_ref[0]
    y = y * (z * jax.nn.sigmoid(z))
    y = y * lax.rsqrt(jnp.mean(y * y, axis=-1, keepdims=True) + EPS) * g_ref[...]
    o_ref[0] = y.astype(o_ref.dtype)


def _ssd_post(y2, xs, g3, d_skip, g_ssd):
    bsz, t_len, xw = xs.shape
    tt = _tile(t_len, 256)
    dsk = jnp.repeat(d_skip, SSD_HEAD_DIM).reshape(1, xw)
    return pl.pallas_call(
        _ssd_post_kernel, grid=(bsz, t_len // tt),
        in_specs=[pl.BlockSpec((1, 1, tt, xw), lambda b, t: (0, b, t, 0)),
                  pl.BlockSpec((1, 1, tt, xw), lambda b, t: (1, b, t, 0)),
                  pl.BlockSpec((1, tt, xw), lambda b, t: (b, t, 0)),
                  pl.BlockSpec((1, tt, xw), lambda b, t: (b, t, 0)),
                  pl.BlockSpec((1, xw), lambda b, t: (0, 0)),
                  pl.BlockSpec((1, xw), lambda b, t: (0, 0))],
        out_specs=pl.BlockSpec((1, tt, xw), lambda b, t: (b, t, 0)),
        out_shape=jax.ShapeDtypeStruct((bsz, t_len, xw), BF16),
        compiler_params=_cparams("parallel", "parallel"),
    )(y2, y2, xs, g3, dsk, g_ssd.reshape(1, xw))


def _merge_kernel(h_ref, wg0_ref, wg1_ref, wg2_ref, wg3_ref, ya_ref, yb_ref, yc_ref, yd_ref, wb_ref, o_ref):
    h = h_ref[0]
    acc = None
    wg_refs = (wg0_ref, wg1_ref, wg2_ref, wg3_ref)
    for i, y_ref in enumerate((ya_ref, yb_ref, yc_ref, yd_ref)):
        gate = jax.nn.sigmoid(_dot(h, wg_refs[i][...]))
        term = gate * _dot(y_ref[0], wb_ref[i].astype(BF16))
        acc = term if acc is None else acc + term
    o_ref[0] = acc.astype(o_ref.dtype)


def _merge(h, wg, ys, w_branch, layer):
    bsz, t_len, d = h.shape
    tm = _tile(t_len, 512)
    tn = _tile(d, 512)
    y_spec = pl.BlockSpec((1, tm, BRANCH_W), lambda n, b, t: (b, t, 0))
    npb = d // tn

    def wg_spec(i):
        return pl.BlockSpec((d, tn), lambda n, b, t: (0, i * npb + n))

    return pl.pallas_call(
        _merge_kernel, grid=(npb, bsz, t_len // tm),
        in_specs=[pl.BlockSpec((1, tm, d), lambda n, b, t: (b, t, 0)),
                  wg_spec(0), wg_spec(1), wg_spec(2), wg_spec(3),
                  y_spec, y_spec, y_spec, y_spec,
                  pl.BlockSpec((None, N_BRANCH, BRANCH_W, tn), lambda n, b, t: (layer, 0, 0, n))],
        out_specs=pl.BlockSpec((1, tm, tn), lambda n, b, t: (b, t, n)),
        out_shape=jax.ShapeDtypeStruct((bsz, t_len, d), BF16),
        compiler_params=_cparams("parallel", "parallel", "parallel"),
    )(h, wg, wg, wg, wg, *ys, w_branch)


def _moe_kernel(item_e_ref, item_b0_ref, item_cnt_ref, nitems_ref, tok_ref, dst_ref, x_hbm, wg_ref, wu_ref,
                bg_ref, bu_ref, wd_ref, bd_ref, y_hbm, xbuf, xb, acc, gsem, ssem, *, sub, nf, n_grid_items, dump0):
    i = pl.program_id(0)
    f = pl.program_id(1)
    ni = nitems_ref[0]
    cnt = item_cnt_ref[i]
    prev = jnp.maximum(i - 1, 0)
    nxt = jnp.minimum(i + 1, n_grid_items - 1)
    prev_cnt = jnp.where(i == 0, 1, item_cnt_ref[prev])
    prev_base = item_b0_ref[prev] * sub

    def start_gather(base, row):
        pltpu.make_async_copy(x_hbm.at[pl.ds(tok_ref[base + row], 1)], xbuf.at[pl.ds(row, 1)], gsem).start()

    def gather_sub(base, hh):
        for r in range(sub):
            start_gather(base, hh * sub + r)

    def scatter_sub(base, hh):
        for r in range(sub):
            row = hh * sub + r
            dst = jnp.where(i == 0, dump0 + r, dst_ref[base + row])
            pltpu.make_async_copy(acc.at[pl.ds(row, 1)], y_hbm.at[pl.ds(dst, 1)], ssem).start()

    def gather_wait(n_sub):
        pltpu.make_async_copy(x_hbm.at[pl.ds(0, sub)], xbuf.at[pl.ds(0, sub)], gsem).wait()

        @pl.when(n_sub == 2)
        def _():
            pltpu.make_async_copy(x_hbm.at[pl.ds(0, sub)], xbuf.at[pl.ds(0, sub)], gsem).wait()

    def scatter_wait(n_sub):
        pltpu.make_async_copy(acc.at[pl.ds(0, sub)], y_hbm.at[pl.ds(0, sub)], ssem).wait()

        @pl.when(n_sub == 2)
        def _():
            pltpu.make_async_copy(acc.at[pl.ds(0, sub)], y_hbm.at[pl.ds(0, sub)], ssem).wait()

    @pl.when(f == 0)
    def _():
        @pl.when(i == 0)
        def _():
            acc[0:sub] = jnp.zeros((sub, acc.shape[1]), acc.dtype)

            def body(r, carry):
                start_gather(0, r)
                return carry

            lax.fori_loop(0, cnt * sub, body, 0)

        @pl.when(i <= ni)
        def _():
            scatter_sub(prev_base, 0)

            @pl.when(prev_cnt == 2)
            def _():
                scatter_sub(prev_base, 1)

        @pl.when(i < ni)
        def _():
            gather_wait(cnt)
            xb[0:sub] = xbuf[0:sub].astype(BF16)

            @pl.when(cnt == 2)
            def _():
                xb[sub:2 * sub] = xbuf[sub:2 * sub].astype(BF16)

            @pl.when(i + 1 < ni)
            def _():
                next_base = item_b0_ref[nxt] * sub
                gather_sub(next_base, 0)

                @pl.when(item_cnt_ref[nxt] == 2)
                def _():
                    gather_sub(next_base, 1)

        @pl.when(i == ni)
        def _():
            scatter_wait(prev_cnt)

    @pl.when(i < ni)
    def _():
        wg = wg_ref[...].astype(BF16)
        wu = wu_ref[...].astype(BF16)
        wd = wd_ref[...].astype(BF16)

        def sub_block(hh):
            rows = slice(hh * sub, (hh + 1) * sub)
            x = xb[rows]
            g = _dot(x, wg) + bg_ref[...]
            u = _dot(x, wu) + bu_ref[...]
            g = jnp.minimum(g, SWIGLU_LIMIT)
            u = jnp.clip(u, -SWIGLU_LIMIT, SWIGLU_LIMIT)
            act = g * jax.nn.sigmoid(SWIGLU_ALPHA * g) * (u + 1.0)
            part = _dot(act.astype(BF16), wd)

            @pl.when(f == 0)
            def _():
                if hh == 0:
                    scatter_wait(prev_cnt)
                acc[rows] = part + bd_ref[...]

            @pl.when(f > 0)
            def _():
                acc[rows] += part

        sub_block(0)

        @pl.when(cnt == 2)
        def _():
            sub_block(1)


def _moe_blocks(h2, items, slot_tok, slot_dst, n_rows_out, w_gu, b_gu, w_dn, b_dn, layer):
    item_e, item_b0, item_cnt, nitems = items
    d = h2.shape[1]
    n_exp, _, ff2 = w_gu.shape[1:]
    ff = ff2 // 2
    sub, fc = MOE_TOKENS, _tile(ff, MOE_FF_CHUNK)
    nf = ff // fc
    n_grid_items = item_e.shape[0]

    def fcl(i, f, ni):
        return jnp.where(i < ni[0], f, nf - 1)

    grid_spec = pltpu.PrefetchScalarGridSpec(
        num_scalar_prefetch=6, grid=(n_grid_items, nf),
        in_specs=[
            pl.BlockSpec(memory_space=pl.ANY),
            pl.BlockSpec((None, None, d, fc), lambda i, f, ie, b0, ct, ni, *_: (layer, ie[i], 0, fcl(i, f, ni))),
            pl.BlockSpec((None, None, d, fc), lambda i, f, ie, b0, ct, ni, *_: (layer, ie[i], 0, nf + fcl(i, f, ni))),
            pl.BlockSpec((None, None, 1, fc), lambda i, f, ie, b0, ct, ni, *_: (layer, ie[i], 0, fcl(i, f, ni))),
            pl.BlockSpec((None, None, 1, fc), lambda i, f, ie, b0, ct, ni, *_: (layer, ie[i], 0, nf + fcl(i, f, ni))),
            pl.BlockSpec((None, None, fc, d), lambda i, f, ie, b0, ct, ni, *_: (layer, ie[i], fcl(i, f, ni), 0)),
            pl.BlockSpec((None, None, 1, d), lambda i, f, ie, b0, ct, ni, *_: (layer, ie[i], 0, 0)),
        ],
        out_specs=pl.BlockSpec(memory_space=pl.ANY),
        scratch_shapes=[pltpu.VMEM((MOE_ITEM_SUBS * sub, d), F32), pltpu.VMEM((MOE_ITEM_SUBS * sub, d), BF16),
                        pltpu.VMEM((MOE_ITEM_SUBS * sub, d), F32),
                        pltpu.SemaphoreType.DMA(()), pltpu.SemaphoreType.DMA(())],
    )
    depth = w_gu.shape[0]
    return pl.pallas_call(
        functools.partial(_moe_kernel, sub=sub, nf=nf, n_grid_items=n_grid_items, dump0=n_rows_out - sub),
        grid_spec=grid_spec,
        out_shape=jax.ShapeDtypeStruct((n_rows_out, d), F32),
        compiler_params=_cparams("arbitrary", "arbitrary", vmem_limit=MOE_VMEM_LIMIT),
    )(item_e, item_b0, item_cnt, nitems, slot_tok, slot_dst, h2, w_gu, w_gu, b_gu.reshape(depth, n_exp, 1, ff2),
      b_gu.reshape(depth, n_exp, 1, ff2), w_dn, b_dn.reshape(depth, n_exp, 1, d))


def _moe(h2, logits, w_gu, b_gu, w_dn, b_dn, layer):
    n_tok, d = h2.shape
    n_exp = logits.shape[1]
    sub = MOE_TOKENS
    top_v, top_i = lax.top_k(logits, TOP_K)
    gate_w = jax.nn.softmax(top_v, axis=-1)
    n_asg = n_tok * TOP_K
    flat_e = top_i.reshape(-1)
    onehot = (flat_e[:, None] == jnp.arange(n_exp, dtype=flat_e.dtype)[None, :]).astype(jnp.int32)
    counts = jnp.sum(onehot, axis=0)
    n_sub = (counts + sub - 1) // sub
    pad_ends = jnp.cumsum(n_sub * sub)
    pad_starts = pad_ends - n_sub * sub
    pos = jnp.sum(onehot * (jnp.cumsum(onehot, axis=0) - 1 + pad_starts[None, :]), axis=1).astype(jnp.int32)
    cap = (-(-n_asg // sub) + n_exp) * sub
    slot_asg = jnp.full((cap,), -1, jnp.int32).at[pos].set(jnp.arange(n_asg, dtype=jnp.int32))
    valid = slot_asg >= 0
    slot_tok = jnp.where(valid, slot_asg // TOP_K, 0)
    slot_dst = jnp.where(valid, (slot_asg % TOP_K) * n_tok + slot_asg // TOP_K,
                         n_asg + jnp.arange(cap, dtype=jnp.int32) % sub)
    per_e = (n_sub + MOE_ITEM_SUBS - 1) // MOE_ITEM_SUBS
    item_ends = jnp.cumsum(per_e)
    nitems = item_ends[-1].astype(jnp.int32)
    n_grid_items = (cap // sub + n_exp * (MOE_ITEM_SUBS - 1)) // MOE_ITEM_SUBS + 1
    idx = jnp.arange(n_grid_items, dtype=jnp.int32)
    e_of = jnp.minimum(jnp.sum((item_ends[None, :] <= idx[:, None]).astype(jnp.int32), axis=1), n_exp - 1)
    sel = (e_of[:, None] == jnp.arange(n_exp)[None, :]).astype(jnp.int32)
    first = idx - jnp.sum(sel * (item_ends - per_e)[None, :], axis=1)
    item_b0 = jnp.sum(sel * (pad_starts // sub)[None, :], axis=1) + first * MOE_ITEM_SUBS
    item_cnt = jnp.clip(jnp.sum(sel * n_sub[None, :], axis=1) - first * MOE_ITEM_SUBS, 0, MOE_ITEM_SUBS)
    used = idx < nitems
    last_e = jnp.sum(jnp.where(idx == nitems - 1, e_of, 0))
    items = (jnp.where(used, e_of, last_e).astype(jnp.int32), jnp.where(used, item_b0, 0).astype(jnp.int32),
             jnp.where(used, item_cnt, 0).astype(jnp.int32), nitems.reshape(1))
    y4 = _moe_blocks(h2, items, slot_tok, slot_dst, n_asg + sub, w_gu, b_gu, w_dn, b_dn, layer)
    return y4, gate_w


def _combine_kernel(x_ref, g_ref, w_ref, y0_ref, y1_ref, y2_ref, y3_ref, o_ref):
    w = w_ref[...]
    m = None
    for j, y_ref in enumerate((y0_ref, y1_ref, y2_ref, y3_ref)):
        term = w[:, j:j + 1] * y_ref[...]
        m = term if m is None else m + term
    o_ref[0] = x_ref[0] + g_ref[0] * m


def _moe_combine(x, gate, y4, gate_w, row0):
    bsz, t_len, d = x.shape
    n_tok = gate_w.shape[0]
    tt = _tile(t_len, 256)
    tpb = t_len // tt
    t0 = row0 // tt
    nt = n_tok // tt

    def y_spec(j):
        return pl.BlockSpec((tt, d), lambda b, t: (j * nt + t0 + b * tpb + t, 0))

    if gate.shape[0] == 1:
        g_spec = pl.BlockSpec((1, 1, d), lambda b, t: (0, 0, 0))
    else:
        g_spec = pl.BlockSpec((1, 1, d), lambda b, t: (b, 0, 0))
    return pl.pallas_call(
        _combine_kernel, grid=(bsz, tpb),
        in_specs=[pl.BlockSpec((1, tt, d), lambda b, t: (b, t, 0)), g_spec,
                  pl.BlockSpec((tt, TOP_K), lambda b, t: (t0 + b * tpb + t, 0)),
                  y_spec(0), y_spec(1), y_spec(2), y_spec(3)],
        out_specs=pl.BlockSpec((1, tt, d), lambda b, t: (b, t, 0)),
        out_shape=jax.ShapeDtypeStruct((bsz, t_len, d), F32),
        compiler_params=_cparams("parallel", "parallel"),
    )(x, gate, gate_w, y4, y4, y4, y4)


def _token_mixers(h, seq, lw, layer, lam_init, ctx):
    latent = ctx is not None
    bsz, t_len = seq
    flat = h.shape[:2]
    rope = _rope_tables(t_len, BRANCH_W) if latent else None
    qkv = _qkv_proj(h, lw['w_qkv'], lw['qkv_gains'], rope).reshape(bsz, t_len, -1)
    g2 = _mm(h, lw['w_lat']).reshape(bsz, t_len, -1)
    g3 = _mm(h, lw['w_ssd']).reshape(bsz, t_len, -1)

    if latent:
        ya = _na_latent(qkv, qkv, qkv, 2, ctx['na_k'], ctx['na_v'], layer, lw['rpb'], q_col=0, k_col=1)
    else:
        ya = _attention(qkv, qkv, qkv, 2, heads=NA_HEADS, maps=1, pieces=_na_pieces, dv=HEAD_DIM,
                        scale=HEAD_DIM ** -0.5, q_col=0, k_col=1, qk_width=BRANCH_W)

    diff = (lw['lam_q1'], lw['lam_k1'], lw['lam_q2'], lw['lam_k2'], lw['g_df_sub'], lam_init)
    df_ctx = (ctx['df_k'], ctx['df_v'], layer, 0) if latent else None
    yb = _attention(qkv, qkv, qkv, 5, heads=DF_HEADS, maps=2, pieces=_df_pieces, dv=2 * HEAD_DIM,
                    scale=HEAD_DIM ** -0.5, ctx=df_ctx, diff=diff, q_col=3, k_col=4, qk_width=BRANCH_W)

    consts = _mla_consts()
    qc, kc, vc, ckv, krope = _mla_prep(g2, lw['g_mla_cq'], lw['g_mla_ckv'], lw['w_uq'], lw['w_ukv'],
                                       lw['g_mla_q'], lw['g_mla_k'], consts, rope)
    mla_ctx = None
    if latent:
        kcx, vcx = _mla_ctx(ctx['mla_ckv'], ctx['mla_krope'], layer, lw['w_ukv'], lw['g_mla_k'], consts)
        mla_ctx = (kcx, vcx, 0, 0)
    yc = _attention(qc, kc, vc, 0, heads=MLA_HEADS, maps=1, pieces=_mla_pieces, dv=V_DIM,
                    scale=MLA_QK ** -0.5, ctx=mla_ctx)

    xs, bm, cm, dt = _ssd_prep(g3, lw['conv_w'], lw['conv_b'], lw['dt_bias'])
    y2, s_fin = _ssd_scan(xs, bm, cm, dt, lw['a_log'], ctx['ssd'] if latent else None, layer)
    yd = _ssd_post(y2, xs, g3, lw['d_skip'], lw['g_ssd'])

    ys = tuple(y.reshape(flat + (BRANCH_W,)) for y in (ya, yb, yc, yd))
    merged = _merge(h, lw['w_gate'], ys, lw['w_branch'], layer)
    if latent:
        return merged, None

    def group(n, *head_shape):
        return qkv[..., n * BRANCH_W:(n + 1) * BRANCH_W].reshape((bsz, t_len) + head_shape)

    new = dict(
        na_k=group(1, NA_HEADS, HEAD_DIM), na_v=group(2, NA_HEADS, HEAD_DIM),
        df_k=group(4, DF_HEADS, 2, HEAD_DIM), df_v=group(5, DF_HEADS, 2 * HEAD_DIM),
        mla_ckv=ckv, mla_krope=krope, ssd=s_fin)
    return merged, new


def kernel(x_prompt, x_sample, cache_na_k, cache_na_v, cache_df_k, cache_df_v, cache_mla_ckv, cache_mla_krope, state_ssd, c, c_ctx, w_mod, b_mod, g_norm1, g_norm2, w_in, g_na_q, g_na_k, rpb, g_df_q, g_df_k, lam_q1, lam_k1, lam_q2, lam_k2, g_df_sub, g_mla_cq, g_mla_ckv, w_uq, w_ukv, g_mla_q, g_mla_k, conv_w, conv_b, dt_bias, a_log, d_skip, g_ssd, w_branch, w_o, w_router, b_router, w_gu, b_gu, w_dn, b_dn):
    depth, d, _ = w_in.shape
    dec_b, _, past = cache_na_k.shape[:3]
    n_p = x_prompt.shape[0] * x_prompt.shape[1]
    n_exp = w_router.shape[-1]
    xp, xs = x_prompt.reshape(1, n_p, d), x_sample

    n_rows = -(-(dec_b + 1) // 8) * 8
    cc = jnp.zeros((n_rows, d), F32).at[:dec_b].set(c).at[dec_b].set(c_ctx)
    mods = _mod_params(cc, w_mod, b_mod)

    ctx_all = dict(
        na_k=cache_na_k.reshape(dec_b, depth, past, NA_HEADS * HEAD_DIM),
        na_v=cache_na_v.reshape(dec_b, depth, past, NA_HEADS * HEAD_DIM),
        df_k=cache_df_k.reshape(dec_b, depth, past, DF_HEADS * 2 * HEAD_DIM),
        df_v=cache_df_v.reshape(dec_b, depth, past, DF_HEADS * 2 * HEAD_DIM),
        mla_ckv=cache_mla_ckv, mla_krope=cache_mla_krope, ssd=state_ssd)

    o_lat = 6 * BRANCH_W
    o_ssd = o_lat + Q_RANK + KV_RANK + ROPE_DIM
    o_gate = o_ssd + BRANCH_W + conv_w.shape[-1] + 2 * SSD_HEADS
    ne_pad = -(-n_exp // LANE) * LANE

    names = ('na_k', 'na_v', 'df_k', 'df_v', 'mla_ckv', 'mla_krope', 'ssd')
    new = {n: [] for n in names}
    for l in range(depth):
        lam_init = 0.8 - 0.6 * math.exp(-0.3 * l)
        wq, wkv, gq, gk = _mla_perm_weights(w_uq[l], w_ukv[l], g_mla_q[l], g_mla_k[l])
        w_l = w_in[l]
        lw = dict(
            w_qkv=w_l[:, :o_lat].astype(BF16), w_lat=w_l[:, o_lat:o_ssd].astype(BF16),
            w_ssd=w_l[:, o_ssd:o_gate].astype(BF16),
            w_gate=w_l[:, o_gate:].astype(BF16),
            qkv_gains=jnp.stack([jnp.tile(g, BRANCH_W // HEAD_DIM) for g in (
                g_na_q[l], g_na_k[l], jnp.ones_like(g_na_q[l]), g_df_q[l], g_df_k[l], jnp.ones_like(g_df_q[l])
            )]).reshape(QKV_GROUPS, 1, BRANCH_W),
            rpb=rpb[l], lam_q1=lam_q1[l], lam_k1=lam_k1[l], lam_q2=lam_q2[l], lam_k2=lam_k2[l], g_df_sub=g_df_sub[l],
            g_mla_cq=g_mla_cq[l], g_mla_ckv=g_mla_ckv[l], w_uq=wq, w_ukv=wkv, g_mla_q=gq, g_mla_k=gk,
            conv_w=conv_w[l], conv_b=conv_b[l], dt_bias=dt_bias[l], a_log=a_log[l], d_skip=d_skip[l],
            g_ssd=g_ssd[l], w_branch=w_branch)
        mc = [m[:, None, :] for m in jnp.split(mods[l, :dec_b], 6, axis=-1)]
        mx = [m[:, None, :] for m in jnp.split(mods[l, dec_b:dec_b + 1], 6, axis=-1)]

        hp = _norm_mod(xp, g_norm1[l], mx[0], mx[1])
        mp, ctx_new = _token_mixers(hp, x_prompt.shape[:2], lw, l, lam_init, None)
        xp = _mm_residual(mp, w_o, l, xp, mx[2])
        hs = _norm_mod(xs, g_norm1[l], mc[0], mc[1])
        ms, _ = _token_mixers(hs, x_sample.shape[:2], lw, l, lam_init, ctx_all)
        xs = _mm_residual(ms, w_o, l, xs, mc[2])

        wr = jnp.zeros((d, ne_pad), F32).at[:, :n_exp].set(w_router[l])
        br = jnp.zeros((1, ne_pad), F32).at[0, :n_exp].set(b_router[l])
        hp2, lg_p = _norm_mod(xp, g_norm2[l], mx[3], mx[4], router=(wr, br))
        hs2, lg_s = _norm_mod(xs, g_norm2[l], mc[3], mc[4], router=(wr, br))
        h2 = jnp.concatenate([hp2.reshape(-1, d), hs2.reshape(-1, d)], axis=0)
        logits = jnp.concatenate([lg_p.reshape(-1, ne_pad), lg_s.reshape(-1, ne_pad)], axis=0)[:, :n_exp]
        y4, gate_w = _moe(h2, logits, w_gu, b_gu, w_dn, b_dn, l)
        xp = _moe_combine(xp, mx[5], y4, gate_w, 0)
        xs = _moe_combine(xs, mc[5], y4, gate_w, n_p)
        for n in names:
            new[n].append(ctx_new[n])
    return (xp.reshape(x_prompt.shape), xs) + tuple(jnp.stack(new[n], axis=1) for n in names)
```

```python
import functools
import math

import jax
import jax.numpy as jnp
from jax import lax
import numpy as np
from jax.experimental import pallas as pl
from jax.experimental.pallas import tpu as pltpu

F32 = jnp.float32
BF16 = jnp.bfloat16

GRID_W = 64
HEAD_DIM = 64
ROPE_BASE = 10000.0
EPS = 1e-6
NA_HEADS = 8
WIN_R = 8
WIN_C = 16
DF_HEADS = 4
MLA_HEADS = 4
Q_RANK = 384
KV_RANK = 128
NOPE_DIM = 128
ROPE_DIM = 64
V_DIM = 128
MLA_QK = NOPE_DIM + ROPE_DIM
SSD_HEADS = 8
SSD_HEAD_DIM = 64
SSD_GROUPS = 2
D_STATE = 128
CHUNK = 128
TOP_K = 4
SWIGLU_ALPHA = 1.702
SWIGLU_LIMIT = 7.0
N_BRANCH = 4
BRANCH_W = 512

V7X_VMEM_BYTES = 64 * 1024 * 1024
VMEM_LIMIT = V7X_VMEM_BYTES - 8 * 1024 * 1024
MOE_VMEM_LIMIT = V7X_VMEM_BYTES - 2 * 1024 * 1024
LANE = 128

NA_ROWS_PER_STEP = 8
MOE_TOKENS = 512
MOE_ITEM_SUBS = 2
MOE_FF_CHUNK = 512


def _cparams(*sem, vmem_limit=VMEM_LIMIT):
    return pltpu.CompilerParams(dimension_semantics=sem, vmem_limit_bytes=vmem_limit)


def _tile(n, pref):
    t = min(n, pref)
    while n % t:
        t //= 2
    return t


def _dot(a, b):
    return jnp.dot(a, b, preferred_element_type=F32)


def _dot_nt(a, b):
    return lax.dot_general(a, b, (((1,), (1,)), ((), ())), preferred_element_type=F32)


def _dot_tn(a, b):
    return lax.dot_general(a, b, (((0,), (0,)), ((), ())), preferred_element_type=F32)


def _split3(x):
    h = x.astype(BF16)
    r = x - h.astype(F32)
    m = r.astype(BF16)
    lo = (r - m.astype(F32)).astype(BF16)
    return h, m, lo


def _sel_dot(x, sel):
    h, m, lo = _split3(x)
    return _dot(h, sel) + _dot(m, sel) + _dot(lo, sel)


def _dot_hi(a, b):
    a1, a2, _ = _split3(a)
    b1, b2, _ = _split3(b)
    return _dot(a1, b1) + (_dot(a1, b2) + _dot(a2, b1))


def _rope_rotate(y, cos, sin):
    w = y.shape[-1]
    lane = lax.broadcasted_iota(jnp.int32, y.shape, 1)
    partner = jnp.where((lane % 32) < 16, pltpu.roll(y, w - 16, 1), pltpu.roll(y, 16, 1))
    return y * cos + partner * sin


def _rope_tables(t_len, width):
    half = HEAD_DIM // 2
    inv = jnp.asarray(ROPE_BASE ** (-np.arange(0, half, 2) / half), F32)
    t = np.arange(t_len)
    cols = []
    sins = []
    for pos in (t // GRID_W, t % GRID_W):
        ang = jnp.asarray(pos, F32)[:, None] * inv[None, :]
        c, s = jnp.cos(ang), jnp.sin(ang)
        cols += [c, c]
        sins += [-s, s]
    cos = jnp.concatenate(cols, axis=-1)
    sin = jnp.concatenate(sins, axis=-1)
    rep = width // HEAD_DIM
    return jnp.tile(cos, (1, rep)), jnp.tile(sin, (1, rep))


def _group_selector(width, group):
    g = np.arange(width) // group
    return jnp.asarray(g[:, None] == g[None, :], BF16)


def _mod_kernel(c_ref, w_ref, b_ref, o_ref):
    c = c_ref[...]
    s = (c * jax.nn.sigmoid(c)).astype(BF16)
    o_ref[0] = _dot(s, w_ref[0].astype(BF16)) + b_ref[0]


def _mod_params(cc, w_mod, b_mod):
    depth, d, n = w_mod.shape
    rows = cc.shape[0]
    tn = _tile(n, 1024)
    return pl.pallas_call(
        _mod_kernel,
        grid=(depth, n // tn),
        in_specs=[pl.BlockSpec((rows, d), lambda l, j: (0, 0)),
                  pl.BlockSpec((1, d, tn), lambda l, j: (l, 0, j)),
                  pl.BlockSpec((1, 1, tn), lambda l, j: (l, 0, j))],
        out_specs=pl.BlockSpec((1, rows, tn), lambda l, j: (l, 0, j)),
        out_shape=jax.ShapeDtypeStruct((depth, rows, n), F32),
        compiler_params=_cparams("parallel", "parallel"),
    )(cc, w_mod, b_mod.reshape(depth, 1, n))


def _norm_mod_body(x_ref, g_ref, sh_ref, sc_ref):
    x = x_ref[0]
    y = x * lax.rsqrt(jnp.mean(x * x, axis=-1, keepdims=True) + EPS)
    y = y * g_ref[...]
    return y * (1.0 + sc_ref[0]) + sh_ref[0]


def _norm_mod_kernel(x_ref, g_ref, sh_ref, sc_ref, o_ref):
    o_ref[0] = _norm_mod_body(x_ref, g_ref, sh_ref, sc_ref).astype(o_ref.dtype)


def _norm_mod_router_kernel(x_ref, g_ref, sh_ref, sc_ref, wr_ref, br_ref, o_ref, lg_ref):
    y = _norm_mod_body(x_ref, g_ref, sh_ref, sc_ref)
    o_ref[0] = y.astype(o_ref.dtype)
    lg_ref[0] = _dot_hi(y, wr_ref[...]) + br_ref[...]


def _mod_spec(m, d):
    if m.shape[0] == 1:
        return pl.BlockSpec((1, 1, d), lambda b, t: (0, 0, 0))
    return pl.BlockSpec((1, 1, d), lambda b, t: (b, 0, 0))


def _norm_mod(x, g, shift, scale, router=None):
    bsz, t_len, d = x.shape
    tt = _tile(t_len, 512)
    in_specs = [pl.BlockSpec((1, tt, d), lambda b, t: (b, t, 0)),
                pl.BlockSpec((1, d), lambda b, t: (0, 0)),
                _mod_spec(shift, d), _mod_spec(scale, d)]
    out_spec = pl.BlockSpec((1, tt, d), lambda b, t: (b, t, 0))
    out_shape = jax.ShapeDtypeStruct((bsz, t_len, d), BF16 if router is None else F32)
    args = [x, g.reshape(1, d), shift, scale]
    if router is None:
        return pl.pallas_call(
            _norm_mod_kernel, grid=(bsz, t_len // tt), in_specs=in_specs, out_specs=out_spec,
            out_shape=out_shape, compiler_params=_cparams("parallel", "parallel"))(*args)
    wr, br = router
    ne = wr.shape[1]
    in_specs += [pl.BlockSpec((d, ne), lambda b, t: (0, 0)), pl.BlockSpec((1, ne), lambda b, t: (0, 0))]
    return pl.pallas_call(
        _norm_mod_router_kernel, grid=(bsz, t_len // tt), in_specs=in_specs,
        out_specs=[out_spec, pl.BlockSpec((1, tt, ne), lambda b, t: (b, t, 0))],
        out_shape=[out_shape, jax.ShapeDtypeStruct((bsz, t_len, ne), F32)],
        compiler_params=_cparams("parallel", "parallel"))(*args, wr, br)


def _mm_kernel(x_ref, w_ref, o_ref):
    o_ref[0] = _dot(x_ref[0].astype(BF16), w_ref[...].astype(BF16)).astype(o_ref.dtype)


def _mm_res_kernel(x_ref, w_ref, r_ref, g_ref, o_ref):
    o_ref[0] = r_ref[0] + g_ref[0] * _dot(x_ref[0].astype(BF16), w_ref[...].astype(BF16))


def _w_spec(w, layer, k, tn):
    if w.ndim == 3:
        return pl.BlockSpec((None, k, tn), lambda n, b, t: (layer, 0, n))
    return pl.BlockSpec((k, tn), lambda n, b, t: (0, n))


def _mm(x, w, layer=0, out_dtype=F32, tm_pref=1024, tn_pref=512):
    bsz, t_len, k = x.shape
    n = w.shape[-1]
    tm = _tile(t_len, tm_pref)
    tn = n if n % LANE else _tile(n, tn_pref)
    return pl.pallas_call(
        _mm_kernel,
        grid=(n // tn, bsz, t_len // tm),
        in_specs=[pl.BlockSpec((1, tm, k), lambda n_, b, t: (b, t, 0)), _w_spec(w, layer, k, tn)],
        out_specs=pl.BlockSpec((1, tm, tn), lambda n_, b, t: (b, t, n_)),
        out_shape=jax.ShapeDtypeStruct((bsz, t_len, n), out_dtype),
        compiler_params=_cparams("parallel", "parallel", "parallel"),
    )(x, w)


def _mm_residual(x, w, layer, res, gate):
    bsz, t_len, k = x.shape
    n = w.shape[-1]
    tm = _tile(t_len, 1024)
    tn = _tile(n, 512)
    if gate.shape[0] == 1:
        g_spec = pl.BlockSpec((1, 1, tn), lambda n_, b, t: (0, 0, n_))
    else:
        g_spec = pl.BlockSpec((1, 1, tn), lambda n_, b, t: (b, 0, n_))
    return pl.pallas_call(
        _mm_res_kernel,
        grid=(n // tn, bsz, t_len // tm),
        in_specs=[pl.BlockSpec((1, tm, k), lambda n_, b, t: (b, t, 0)), _w_spec(w, layer, k, tn),
                  pl.BlockSpec((1, tm, tn), lambda n_, b, t: (b, t, n_)), g_spec],
        out_specs=pl.BlockSpec((1, tm, tn), lambda n_, b, t: (b, t, n_)),
        out_shape=jax.ShapeDtypeStruct((bsz, t_len, n), F32),
        compiler_params=_cparams("parallel", "parallel", "parallel"),
    )(x, w, res, gate)


QKV_GROUPS = 6
QKV_VALUE_GROUPS = (2, 5)
QKV_ROPE_GROUPS = (3, 4)


def _qkv_kernel(*refs, rope):
    if rope:
        x_ref, w_ref, g_ref, e_ref, cos_ref, sin_ref, o_ref = refs
    else:
        x_ref, w_ref, g_ref, e_ref, o_ref = refs
    n = pl.program_id(0)
    y = _dot(x_ref[0], w_ref[...])
    is_value = functools.reduce(jnp.logical_or, [n == v for v in QKV_VALUE_GROUPS])

    @pl.when(is_value)
    def _():
        o_ref[0] = y

    @pl.when(jnp.logical_not(is_value))
    def _():
        ms = _sel_dot(y * y, e_ref[...]) * (1.0 / HEAD_DIM)
        yn = y * lax.rsqrt(ms + EPS) * g_ref[0]
        if rope:
            is_rope = functools.reduce(jnp.logical_or, [n == v for v in QKV_ROPE_GROUPS])

            @pl.when(is_rope)
            def _():
                o_ref[0] = _rope_rotate(yn, cos_ref[...], sin_ref[...])

            @pl.when(jnp.logical_not(is_rope))
            def _():
                o_ref[0] = yn
        else:
            o_ref[0] = yn


def _qkv_proj(h, w_qkv, gains, rope=None):
    bsz, t_len, k = h.shape
    w = BRANCH_W
    tm = _tile(t_len, 1024)
    in_specs = [pl.BlockSpec((1, tm, k), lambda n, b, t: (b, t, 0)),
                pl.BlockSpec((k, w), lambda n, b, t: (0, n)),
                pl.BlockSpec((1, 1, w), lambda n, b, t: (n, 0, 0)),
                pl.BlockSpec((w, w), lambda n, b, t: (0, 0))]
    args = [h, w_qkv, gains, _group_selector(w, HEAD_DIM)]
    if rope is not None:
        in_specs += [pl.BlockSpec((tm, w), lambda n, b, t: (t, 0))] * 2
        args += list(rope)
    return pl.pallas_call(
        functools.partial(_qkv_kernel, rope=rope is not None),
        grid=(QKV_GROUPS, bsz, t_len // tm), in_specs=in_specs,
        out_specs=pl.BlockSpec((1, tm, w), lambda n, b, t: (b, t, n)),
        out_shape=jax.ShapeDtypeStruct((bsz, t_len, QKV_GROUPS * w), F32),
        compiler_params=_cparams("parallel", "parallel", "parallel"),
    )(*args)


def _attn_kernel(*refs, heads, maps, pieces, dv, scale, has_ctx, diff, post_scale):
    it = iter(refs)
    q_ref, k_ref, v_ref = next(it), next(it), next(it)
    kc_ref = vc_ref = None
    if has_ctx:
        kc_ref, vc_ref = next(it), next(it)
    if diff is not None:
        lq1, lk1, lq2, lk2, gsub_ref = next(it), next(it), next(it), next(it), next(it)
    o_ref = next(it)

    q = q_ref[0].astype(BF16)
    k = k_ref[0].astype(BF16)
    v = v_ref[0].astype(BF16)
    if has_ctx:
        kc = kc_ref[0, 0].astype(BF16)
        vc = vc_ref[0, 0].astype(BF16)
    if diff is not None:
        lam = (jnp.exp(jnp.sum(lq1[...] * lk1[...], axis=-1, keepdims=True))
               - jnp.exp(jnp.sum(lq2[...] * lk2[...], axis=-1, keepdims=True)) + diff)

    for h in range(heads):
        w_own = w_ctx = None
        for m in range(maps):
            s = s_c = None
            for (qo, ko, d) in pieces(h, m):
                part = _dot_nt(q[:, qo:qo + d], k[:, ko:ko + d])
                s = part if s is None else s + part
                if has_ctx:
                    part_c = _dot_nt(q[:, qo:qo + d], kc[:, ko:ko + d])
                    s_c = part_c if s_c is None else s_c + part_c
            s = s * scale
            mx = jnp.max(s, axis=-1, keepdims=True)
            if has_ctx:
                s_c = s_c * scale
                mx = jnp.maximum(mx, jnp.max(s_c, axis=-1, keepdims=True))
            p = jnp.exp(s - mx)
            den = jnp.sum(p, axis=-1, keepdims=True)
            if has_ctx:
                p_c = jnp.exp(s_c - mx)
                den = den + jnp.sum(p_c, axis=-1, keepdims=True)
            inv = 1.0 / den
            if m == 1:
                inv = -lam * inv
            w_own = p * inv if w_own is None else w_own + p * inv
            if has_ctx:
                w_ctx = p_c * inv if w_ctx is None else w_ctx + p_c * inv
        o = _dot(w_own.astype(BF16), v[:, h * dv:(h + 1) * dv])
        if has_ctx:
            o = o + _dot(w_ctx.astype(BF16), vc[:, h * dv:(h + 1) * dv])
        if diff is not None:
            o = o * lax.rsqrt(jnp.mean(o * o, axis=-1, keepdims=True) + EPS) * gsub_ref[...] * post_scale
        o_ref[0, :, h * dv:(h + 1) * dv] = o.astype(o_ref.dtype)


def _attention(q, k, v, v_col, *, heads, maps, pieces, dv, scale, ctx=None, diff=None, q_col=0, k_col=0,
               qk_width=None):
    bsz, t_len, _ = q.shape
    wq = wk = qk_width if qk_width is not None else q.shape[-1]
    wv = heads * dv
    tq = _tile(t_len, 256)
    in_specs = [pl.BlockSpec((1, tq, wq), lambda b, t: (b, t, q_col)),
                pl.BlockSpec((1, t_len, wk), lambda b, t: (b, 0, k_col)),
                pl.BlockSpec((1, t_len, wv), lambda b, t: (b, 0, v_col))]
    args = [q, k, v]
    if ctx is not None:
        kc, vc, layer, vc_col = ctx
        tc = kc.shape[2]
        in_specs += [pl.BlockSpec((1, 1, tc, wk), lambda b, t: (b, layer, 0, 0)),
                     pl.BlockSpec((1, 1, tc, wv), lambda b, t: (b, layer, 0, vc_col))]
        args += [kc, vc]
    lam_init = 0.0
    if diff is not None:
        lq1, lk1, lq2, lk2, gsub, lam_init = diff
        for a in (lq1, lk1, lq2, lk2, gsub):
            in_specs.append(pl.BlockSpec((1, a.shape[0]), lambda b, t: (0, 0)))
            args.append(a.reshape(1, -1))
    kern = functools.partial(
        _attn_kernel, heads=heads, maps=maps, pieces=pieces, dv=dv, scale=scale,
        has_ctx=ctx is not None, diff=lam_init if diff is not None else None,
        post_scale=1.0 - lam_init)
    return pl.pallas_call(
        kern, grid=(bsz, t_len // tq), in_specs=in_specs,
        out_specs=pl.BlockSpec((1, tq, wv), lambda b, t: (b, t, 0)),
        out_shape=jax.ShapeDtypeStruct((bsz, t_len, wv), BF16),
        compiler_params=_cparams("parallel", "parallel"),
    )(*args)


def _na_pieces(h, m):
    return [(h * HEAD_DIM, h * HEAD_DIM, HEAD_DIM)]


def _df_pieces(h, m):
    o = (2 * h + m) * HEAD_DIM
    return [(o, o, HEAD_DIM)]


def _mla_pieces(h, m):
    nope_w = MLA_HEADS * NOPE_DIM
    return [(h * NOPE_DIM, h * NOPE_DIM, NOPE_DIM),
            (nope_w + h * ROPE_DIM, nope_w + h * ROPE_DIM, ROPE_DIM)]


def _na_latent_kernel(q_ref, k_ref, v_ref, kc_ref, vc_ref, bias_ref, o_ref, *, rows, kr, rps, scale):
    kc = kc_ref[0, 0].astype(BF16)
    vc = vc_ref[0, 0].astype(BF16)
    for rr in range(rps):
        r = pl.program_id(1) * rps + rr
        start_row = jnp.clip(r - WIN_R // 2, 0, rows - kr)
        start = pl.multiple_of(start_row * GRID_W, GRID_W)
        q = q_ref[0, rr * GRID_W:(rr + 1) * GRID_W, :].astype(BF16)
        kl = k_ref[0, pl.ds(start, kr * GRID_W), :].astype(BF16)
        vl = v_ref[0, pl.ds(start, kr * GRID_W), :].astype(BF16)
        ro0 = start_row - r + WIN_R - 1
        s_heads = []
        for h in range(NA_HEADS):
            sl = slice(h * HEAD_DIM, (h + 1) * HEAD_DIM)
            bias = jnp.concatenate([bias_ref[h, ro0 + 2 * j] for j in range(kr // 2)], axis=-1)
            s_loc = _dot_nt(q[:, sl], kl[:, sl]) * scale + bias
            s_ctx = _dot_nt(q[:, sl], kc[:, sl]) * scale
            s_heads.append(jnp.concatenate([s_loc, s_ctx], axis=-1))
        s = jnp.concatenate(s_heads, axis=0)
        p = jnp.exp(s - jnp.max(s, axis=-1, keepdims=True))
        pn = (p * (1.0 / jnp.sum(p, axis=-1, keepdims=True))).astype(BF16)
        n_loc = kr * GRID_W
        for h in range(NA_HEADS):
            sl = slice(h * HEAD_DIM, (h + 1) * HEAD_DIM)
            ph = pn[h * GRID_W:(h + 1) * GRID_W]
            o = _dot(ph[:, :n_loc], vl[:, sl]) + _dot(ph[:, n_loc:], vc[:, sl])
            o_ref[0, rr * GRID_W:(rr + 1) * GRID_W, sl] = o.astype(o_ref.dtype)


def _na_bias_table(rpb_l):
    qc = np.arange(GRID_W)
    kc = np.arange(GRID_W)
    qstart = np.clip(qc - WIN_C // 2, 0, GRID_W - WIN_C)
    valid = (kc[None, :] >= qstart[:, None]) & (kc[None, :] < qstart[:, None] + WIN_C)
    c_off = np.clip(kc[None, :] - qc[:, None] + WIN_C - 1, 0, 2 * WIN_C - 2)
    onehot = (c_off[None] == np.arange(2 * WIN_C - 1)[:, None, None]) & valid[None]
    tab = jnp.einsum('hrc,cqk->hrqk', rpb_l.astype(F32), jnp.asarray(onehot, F32),
                     precision=lax.Precision.HIGHEST)
    tab = jnp.where(valid[None, None], tab, -jnp.inf)
    return jnp.concatenate([tab[:, :-1], tab[:, 1:]], axis=-1)


def _na_latent(q, k, v, v_col, kc, vc, layer, rpb_l, q_col=0, k_col=0):
    bsz, t_len, _ = q.shape
    w = NA_HEADS * HEAD_DIM
    rows = t_len // GRID_W
    kr = min(WIN_R, rows)
    assert kr % 2 == 0, "the bias table pairs key rows"
    tc = kc.shape[2]
    bias = _na_bias_table(rpb_l)
    rps = _tile(rows, NA_ROWS_PER_STEP)
    return pl.pallas_call(
        functools.partial(_na_latent_kernel, rows=rows, kr=kr, rps=rps, scale=HEAD_DIM ** -0.5),
        grid=(bsz, rows // rps),
        in_specs=[pl.BlockSpec((1, rps * GRID_W, w), lambda b, r: (b, r, q_col)),
                  pl.BlockSpec((1, t_len, w), lambda b, r: (b, 0, k_col)),
                  pl.BlockSpec((1, t_len, w), lambda b, r: (b, 0, v_col)),
                  pl.BlockSpec((1, 1, tc, w), lambda b, r: (b, layer, 0, 0)),
                  pl.BlockSpec((1, 1, tc, w), lambda b, r: (b, layer, 0, 0)),
                  _const_spec(bias)],
        out_specs=pl.BlockSpec((1, rps * GRID_W, w), lambda b, r: (b, r, 0)),
        out_shape=jax.ShapeDtypeStruct((bsz, t_len, w), BF16),
        compiler_params=_cparams("parallel", "parallel"),
    )(q, k, v, kc, vc, bias)


def _mla_k_part(ckv_bf16, krope, wukv_ref, gk_ref, ek_ref, tile_ref, rope_tabs):
    nope_w = MLA_HEADS * NOPE_DIM
    kv = _dot(ckv_bf16, wukv_ref[...])
    k_nope = kv[:, :nope_w]
    kr_t = _sel_dot(krope, tile_ref[...])
    sq = jnp.concatenate([k_nope * k_nope, krope * krope], axis=-1)
    ms = _sel_dot(sq, ek_ref[...]) * (1.0 / MLA_QK)
    rs = lax.rsqrt(ms + EPS)
    g = gk_ref[...]
    kn = k_nope * rs[:, :nope_w] * g[:, :nope_w]
    kro = kr_t * rs[:, nope_w:] * g[:, nope_w:]
    if rope_tabs is not None:
        kro = _rope_rotate(kro, rope_tabs[0][...], rope_tabs[1][...])
    return jnp.concatenate([kn, kro], axis=-1), kv[:, nope_w:]


def _mla_prep_kernel(*refs, rope):
    it = iter(refs)
    x_ref = next(it)
    gcq_ref, gckv_ref, wuq_ref, wukv_ref, gq_ref, gk_ref, eq_ref, ek_ref, tile_ref = (next(it) for _ in range(9))
    tabs = (next(it), next(it)) if rope else None
    q_ref, k_ref, v_ref, ckv_ref, kr_ref = (next(it) for _ in range(5))
    nope_w = MLA_HEADS * NOPE_DIM

    x = x_ref[0]
    dq = x[:, :Q_RANK]
    cq = dq * lax.rsqrt(jnp.mean(dq * dq, axis=-1, keepdims=True) + EPS) * gcq_ref[...]
    dkv = x[:, Q_RANK:Q_RANK + KV_RANK]
    ckv = dkv * lax.rsqrt(jnp.mean(dkv * dkv, axis=-1, keepdims=True) + EPS) * gckv_ref[...]
    krope = x[:, Q_RANK + KV_RANK:]
    ckv_ref[0] = ckv
    kr_ref[0] = krope

    qr = _dot(cq.astype(BF16), wuq_ref[...])
    ms = _sel_dot(qr * qr, eq_ref[...]) * (1.0 / MLA_QK)
    qn = qr * lax.rsqrt(ms + EPS) * gq_ref[...]
    if rope:
        q_ref[0] = jnp.concatenate(
            [qn[:, :nope_w], _rope_rotate(qn[:, nope_w:], tabs[0][...], tabs[1][...])], axis=-1)
    else:
        q_ref[0] = qn
    kn, v = _mla_k_part(ckv.astype(BF16), krope, wukv_ref, gk_ref, ek_ref, tile_ref, tabs)
    k_ref[0] = kn
    v_ref[0] = v.astype(v_ref.dtype)


def _mla_ctx_kernel(ckv_ref, kr_ref, wukv_ref, gk_ref, ek_ref, tile_ref, k_ref, v_ref):
    kn, v = _mla_k_part(ckv_ref[0, 0].astype(BF16), kr_ref[0, 0], wukv_ref, gk_ref, ek_ref, tile_ref, None)
    k_ref[0, 0] = kn
    v_ref[0, 0] = v.astype(v_ref.dtype)


def _mla_consts():
    nope_w = MLA_HEADS * NOPE_DIM
    qk_w = nope_w + MLA_HEADS * ROPE_DIM
    head_q = np.concatenate([np.arange(nope_w) // NOPE_DIM, np.arange(MLA_HEADS * ROPE_DIM) // ROPE_DIM])
    eq = head_q[:, None] == head_q[None, :]
    ek = np.concatenate([eq[:nope_w], np.ones((ROPE_DIM, qk_w), bool)], axis=0)
    tile = np.arange(ROPE_DIM)[:, None] == (np.arange(MLA_HEADS * ROPE_DIM) % ROPE_DIM)[None, :]
    return jnp.asarray(eq, BF16), jnp.asarray(ek, BF16), jnp.asarray(tile, BF16)


def _mla_perm_weights(w_uq_l, w_ukv_l, g_q, g_k):
    wq = w_uq_l.reshape(Q_RANK, MLA_HEADS, MLA_QK)
    wq = jnp.concatenate([wq[:, :, :NOPE_DIM].reshape(Q_RANK, -1), wq[:, :, NOPE_DIM:].reshape(Q_RANK, -1)], axis=1)
    wkv = w_ukv_l.reshape(KV_RANK, MLA_HEADS, NOPE_DIM + V_DIM)
    wkv = jnp.concatenate([wkv[:, :, :NOPE_DIM].reshape(KV_RANK, -1), wkv[:, :, NOPE_DIM:].reshape(KV_RANK, -1)], axis=1)

    def gain(g):
        return jnp.concatenate([jnp.tile(g[:NOPE_DIM], MLA_HEADS), jnp.tile(g[NOPE_DIM:], MLA_HEADS)]).reshape(1, -1)

    return wq.astype(BF16), wkv.astype(BF16), gain(g_q), gain(g_k)


def _const_spec(a):
    nd = a.ndim
    return pl.BlockSpec(a.shape, lambda *_: (0,) * nd)


def _mla_prep(x, g_cq, g_ckv, wq, wkv, gq, gk, consts, rope):
    bsz, t_len, w = x.shape
    tt = _tile(t_len, 512)
    eq, ek, tile = consts
    qk_w = MLA_HEADS * MLA_QK
    args = [x, g_cq.reshape(1, -1), g_ckv.reshape(1, -1), wq, wkv, gq, gk, eq, ek, tile]
    in_specs = [pl.BlockSpec((1, tt, w), lambda b, t: (b, t, 0))] + [_const_spec(a) for a in args[1:]]
    if rope is not None:
        rw = MLA_HEADS * ROPE_DIM
        in_specs += [pl.BlockSpec((tt, rw), lambda b, t: (t, 0))] * 2
        args += [rope[0][:, :rw], rope[1][:, :rw]]
    widths = (qk_w, qk_w, MLA_HEADS * V_DIM, KV_RANK, ROPE_DIM)
    dtypes = (F32, F32, BF16, F32, F32)
    return pl.pallas_call(
        functools.partial(_mla_prep_kernel, rope=rope is not None),
        grid=(bsz, t_len // tt), in_specs=in_specs,
        out_specs=[pl.BlockSpec((1, tt, wd), lambda b, t: (b, t, 0)) for wd in widths],
        out_shape=[jax.ShapeDtypeStruct((bsz, t_len, wd), dt) for wd, dt in zip(widths, dtypes)],
        compiler_params=_cparams("parallel", "parallel"),
    )(*args)


def _mla_ctx(cache_ckv, cache_krope, layer, wkv, gk, consts):
    bsz, depth, tc, _ = cache_ckv.shape
    _, ek, tile = consts
    qk_w = MLA_HEADS * MLA_QK
    vw = MLA_HEADS * V_DIM
    args = [cache_ckv, cache_krope, wkv, gk, ek, tile]
    in_specs = [pl.BlockSpec((1, 1, tc, KV_RANK), lambda b: (b, layer, 0, 0)),
                pl.BlockSpec((1, 1, tc, ROPE_DIM), lambda b: (b, layer, 0, 0))]
    in_specs += [_const_spec(a) for a in args[2:]]
    return pl.pallas_call(
        _mla_ctx_kernel, grid=(bsz,), in_specs=in_specs,
        out_specs=[pl.BlockSpec((1, 1, tc, qk_w), lambda b: (b, 0, 0, 0)),
                   pl.BlockSpec((1, 1, tc, vw), lambda b: (b, 0, 0, 0))],
        out_shape=[jax.ShapeDtypeStruct((bsz, 1, tc, qk_w), F32), jax.ShapeDtypeStruct((bsz, 1, tc, vw), BF16)],
        compiler_params=_cparams("parallel"),
    )(*args)


def _ssd_prep_kernel(x_ref, cw_ref, cb_ref, dtb_ref, xs_ref, bm_ref, cm_ref, dt_ref):
    t_len = x_ref.shape[1]
    cw_total = cw_ref.shape[1]
    xbc = x_ref[0, :, BRANCH_W:BRANCH_W + cw_total]
    row = lax.broadcasted_iota(jnp.int32, xbc.shape, 0)
    prev1 = jnp.where(row >= 1, pltpu.roll(xbc, 1, 0), 0.0)
    next1 = jnp.where(row < t_len - 1, pltpu.roll(xbc, t_len - 1, 0), 0.0)
    next2 = jnp.where(row < t_len - 2, pltpu.roll(xbc, t_len - 2, 0), 0.0)
    cw = cw_ref[...]
    u = prev1 * cw[0:1] + xbc * cw[1:2] + next1 * cw[2:3] + next2 * cw[3:4] + cb_ref[...]
    u = u * jax.nn.sigmoid(u)
    xw = SSD_HEADS * SSD_HEAD_DIM
    bw = SSD_GROUPS * D_STATE
    xs_ref[0] = u[:, :xw]
    bm_ref[0] = u[:, xw:xw + bw]
    cm_ref[0] = u[:, xw + bw:]
    raw = x_ref[0, :, BRANCH_W + cw_total:] + dtb_ref[...]
    dt_ref[0] = jnp.maximum(raw, 0.0) + jnp.log1p(jnp.exp(-jnp.abs(raw)))


def _ssd_prep(g3, conv_w, conv_b, dt_bias):
    bsz, t_len, w = g3.shape
    cw = conv_w.shape[1]
    xw = SSD_HEADS * SSD_HEAD_DIM
    bw = SSD_GROUPS * D_STATE
    nh2 = 2 * SSD_HEADS
    widths = (xw, bw, bw, nh2)
    return pl.pallas_call(
        _ssd_prep_kernel, grid=(bsz,),
        in_specs=[pl.BlockSpec((1, t_len, w), lambda b: (b, 0, 0)),
                  pl.BlockSpec(conv_w.shape, lambda b: (0, 0)),
                  pl.BlockSpec((1, cw), lambda b: (0, 0)),
                  pl.BlockSpec((1, nh2), lambda b: (0, 0))],
        out_specs=[pl.BlockSpec((1, t_len, wd), lambda b: (b, 0, 0)) for wd in widths],
        out_shape=[jax.ShapeDtypeStruct((bsz, t_len, wd), F32) for wd in widths],
        compiler_params=_cparams("parallel"),
    )(g3, conv_w, conv_b.reshape(1, cw), dt_bias.reshape(1, nh2))


def _ssd_scan_kernel(*refs, has_s0, nc):
    if has_s0:
        xs_ref, bm_ref, cm_ref, dt_ref, dtt_ref, alogt_ref, s0_ref, y_ref, sfin_ref, st_ref = refs
    else:
        xs_ref, bm_ref, cm_ref, dt_ref, dtt_ref, alogt_ref, y_ref, sfin_ref, st_ref = refs
    d = pl.program_id(1)
    c = pl.program_id(2)
    chunk = xs_ref.shape[1]

    @pl.when(c == 0)
    def _():
        if has_s0:
            st_ref[...] = s0_ref[0, 0, 0]
        else:
            st_ref[...] = jnp.zeros_like(st_ref)

    a_col = -jnp.exp(alogt_ref[0])
    dt = dt_ref[0, 0]
    dta_t = dtt_ref[0, 0] * a_col
    ri = lax.broadcasted_iota(jnp.int32, (chunk, chunk), 0)
    ci = lax.broadcasted_iota(jnp.int32, (chunk, chunk), 1)
    sign = 1 - 2 * d
    mask = (ri - ci) * sign >= 0
    mask_t = jnp.where((ci - ri) * sign >= 0, 1.0, 0.0).astype(BF16)
    acs_rows = _sel_dot(dta_t, mask_t)

    xs = xs_ref[0]
    rep = SSD_HEADS // SSD_GROUPS
    cb = []
    for g in range(SSD_GROUPS):
        cg = cm_ref[0, :, g * D_STATE:(g + 1) * D_STATE].astype(BF16)
        bg = bm_ref[0, :, g * D_STATE:(g + 1) * D_STATE].astype(BF16)
        cb.append((cg, bg, _dot_nt(cg, bg)))

    heads = range(SSD_HEADS)
    dta_rows = [dta_t[h:h + 1, :] for h in heads]
    acs_cols = [jnp.sum(jnp.where(mask, dta_rows[h], 0.0), axis=-1, keepdims=True) for h in heads]
    tots = [jnp.sum(dta_rows[h], axis=-1, keepdims=True) for h in heads]
    decays = [jnp.exp(jnp.where(mask, acs_cols[h] - acs_rows[h:h + 1, :], -jnp.inf)) for h in heads]
    xdts = [xs[:, h * SSD_HEAD_DIM:(h + 1) * SSD_HEAD_DIM] * dt[:, h:h + 1] for h in heads]
    states = [st_ref[h] for h in heads]
    y_diag = [_dot((cb[h // rep][2] * decays[h]).astype(BF16), xdts[h].astype(BF16)) for h in heads]
    y_off = [_dot_nt(cb[h // rep][0], states[h].astype(BF16)) for h in heads]
    upd = [_dot_tn((xdts[h] * jnp.exp(tots[h] - acs_cols[h])).astype(BF16), cb[h // rep][1]) for h in heads]
    for h in heads:
        st_ref[h] = states[h] * jnp.exp(tots[h]) + upd[h]
        y_ref[0, 0, :, h * SSD_HEAD_DIM:(h + 1) * SSD_HEAD_DIM] = y_diag[h] + jnp.exp(acs_cols[h]) * y_off[h]

    @pl.when(c == nc - 1)
    def _():
        sfin_ref[0, 0] = st_ref[...]


def _ssd_scan(xs, bm, cm, dt, a_log_l, s0=None, layer=0):
    bsz, t_len, xw = xs.shape
    chunk = CHUNK if t_len % CHUNK == 0 else t_len
    nc = t_len // chunk
    nh = SSD_HEADS
    bw = bm.shape[-1]
    dt4 = dt.reshape(bsz, t_len, 2, nh).transpose(0, 2, 1, 3)
    dtt = dt4.transpose(0, 1, 3, 2)

    def tok(b, d, c):
        return c + d * (nc - 1 - 2 * c)

    in_specs = [pl.BlockSpec((1, chunk, xw), lambda b, d, c: (b, tok(b, d, c), 0)),
                pl.BlockSpec((1, chunk, bw), lambda b, d, c: (b, tok(b, d, c), 0)),
                pl.BlockSpec((1, chunk, bw), lambda b, d, c: (b, tok(b, d, c), 0)),
                pl.BlockSpec((1, 1, chunk, nh), lambda b, d, c: (b, d, tok(b, d, c), 0)),
                pl.BlockSpec((1, 1, nh, chunk), lambda b, d, c: (b, d, 0, tok(b, d, c))),
                pl.BlockSpec((1, nh, 1), lambda b, d, c: (d, 0, 0))]
    args = [xs, bm, cm, dt4, dtt, a_log_l.reshape(2, nh, 1)]
    if s0 is not None:
        in_specs.append(pl.BlockSpec((1, 1, 1, nh, SSD_HEAD_DIM, D_STATE), lambda b, d, c: (b, layer, d, 0, 0, 0)))
        args.append(s0)
    return pl.pallas_call(
        functools.partial(_ssd_scan_kernel, has_s0=s0 is not None, nc=nc),
        grid=(bsz, 2, nc), in_specs=in_specs,
        out_specs=[pl.BlockSpec((1, 1, chunk, xw), lambda b, d, c: (d, b, tok(b, d, c), 0)),
                   pl.BlockSpec((1, 1, nh, SSD_HEAD_DIM, D_STATE), lambda b, d, c: (b, d, 0, 0, 0))],
        out_shape=[jax.ShapeDtypeStruct((2, bsz, t_len, xw), F32),
                   jax.ShapeDtypeStruct((bsz, 2, nh, SSD_HEAD_DIM, D_STATE), F32)],
        scratch_shapes=[pltpu.VMEM((nh, SSD_HEAD_DIM, D_STATE), F32)],
        compiler_params=_cparams("parallel", "arbitrary", "arbitrary"),
    )(*args)


def _ssd_post_kernel(yf_ref, yb_ref, xs_ref, z_ref, dsk_ref, g_ref, o_ref):
    z = z_ref[0]
    y = yf_ref[0, 0] + yb_ref[0, 0] + dsk_ref[...] * xs_ref[0]
    y = y * (z * jax.nn.sigmoid(z))
    y = y * lax.rsqrt(jnp.mean(y * y, axis=-1, keepdims=True) + EPS) * g_ref[...]
    o_ref[0] = y.astype(o_ref.dtype)


def _ssd_post(y2, xs, g3, d_skip, g_ssd):
    bsz, t_len, xw = xs.shape
    tt = _tile(t_len, 512)
    dsk = jnp.repeat(d_skip, SSD_HEAD_DIM).reshape(1, xw)
    return pl.pallas_call(
        _ssd_post_kernel, grid=(bsz, t_len // tt),
        in_specs=[pl.BlockSpec((1, 1, tt, xw), lambda b, t: (0, b, t, 0)),
                  pl.BlockSpec((1, 1, tt, xw), lambda b, t: (1, b, t, 0)),
                  pl.BlockSpec((1, tt, xw), lambda b, t: (b, t, 0)),
                  pl.BlockSpec((1, tt, xw), lambda b, t: (b, t, 0)),
                  pl.BlockSpec((1, xw), lambda b, t: (0, 0)),
                  pl.BlockSpec((1, xw), lambda b, t: (0, 0))],
        out_specs=pl.BlockSpec((1, tt, xw), lambda b, t: (b, t, 0)),
        out_shape=jax.ShapeDtypeStruct((bsz, t_len, xw), BF16),
        compiler_params=_cparams("parallel", "parallel"),
    )(y2, y2, xs, g3, dsk, g_ssd.reshape(1, xw))


def _merge_kernel(h_ref, wg0_ref, wg1_ref, wg2_ref, wg3_ref, ya_ref, yb_ref, yc_ref, yd_ref, wb_ref, o_ref):
    h = h_ref[0]
    acc = None
    wg_refs = (wg0_ref, wg1_ref, wg2_ref, wg3_ref)
    for i, y_ref in enumerate((ya_ref, yb_ref, yc_ref, yd_ref)):
        gate = jax.nn.sigmoid(_dot(h, wg_refs[i][...]))
        term = gate * _dot(y_ref[0], wb_ref[i].astype(BF16))
        acc = term if acc is None else acc + term
    o_ref[0] = acc.astype(o_ref.dtype)


def _merge(h, wg, ys, w_branch, layer):
    bsz, t_len, d = h.shape
    tm = _tile(t_len, 512)
    tn = _tile(d, 512)
    y_spec = pl.BlockSpec((1, tm, BRANCH_W), lambda n, b, t: (b, t, 0))
    npb = d // tn

    def wg_spec(i):
        return pl.BlockSpec((d, tn), lambda n, b, t: (0, i * npb + n))

    return pl.pallas_call(
        _merge_kernel, grid=(npb, bsz, t_len // tm),
        in_specs=[pl.BlockSpec((1, tm, d), lambda n, b, t: (b, t, 0)),
                  wg_spec(0), wg_spec(1), wg_spec(2), wg_spec(3),
                  y_spec, y_spec, y_spec, y_spec,
                  pl.BlockSpec((None, N_BRANCH, BRANCH_W, tn), lambda n, b, t: (layer, 0, 0, n))],
        out_specs=pl.BlockSpec((1, tm, tn), lambda n, b, t: (b, t, n)),
        out_shape=jax.ShapeDtypeStruct((bsz, t_len, d), BF16),
        compiler_params=_cparams("parallel", "parallel", "parallel"),
    )(h, wg, wg, wg, wg, *ys, w_branch)


def _moe_kernel(item_e_ref, item_b0_ref, item_cnt_ref, nitems_ref, tok_ref, dst_ref, x_hbm, wg_ref, wu_ref,
                bg_ref, bu_ref, wd_ref, bd_ref, y_hbm, xbuf, xb, acc, gsem, ssem, *, sub, nf, n_grid_items, dump0):
    i = pl.program_id(0)
    f = pl.program_id(1)
    ni = nitems_ref[0]
    cnt = item_cnt_ref[i]
    prev = jnp.maximum(i - 1, 0)
    nxt = jnp.minimum(i + 1, n_grid_items - 1)
    prev_cnt = jnp.where(i == 0, 1, item_cnt_ref[prev])
    prev_base = item_b0_ref[prev] * sub

    def start_gather(base, row):
        pltpu.make_async_copy(x_hbm.at[pl.ds(tok_ref[base + row], 1)], xbuf.at[pl.ds(row, 1)], gsem).start()

    def gather_sub(base, hh):
        for r in range(sub):
            start_gather(base, hh * sub + r)

    def scatter_sub(base, hh):
        for r in range(sub):
            row = hh * sub + r
            dst = jnp.where(i == 0, dump0 + r, dst_ref[base + row])
            pltpu.make_async_copy(acc.at[pl.ds(row, 1)], y_hbm.at[pl.ds(dst, 1)], ssem).start()

    def gather_wait(n_sub):
        pltpu.make_async_copy(x_hbm.at[pl.ds(0, sub)], xbuf.at[pl.ds(0, sub)], gsem).wait()

        @pl.when(n_sub == 2)
        def _():
            pltpu.make_async_copy(x_hbm.at[pl.ds(0, sub)], xbuf.at[pl.ds(0, sub)], gsem).wait()

    def scatter_wait(n_sub):
        pltpu.make_async_copy(acc.at[pl.ds(0, sub)], y_hbm.at[pl.ds(0, sub)], ssem).wait()

        @pl.when(n_sub == 2)
        def _():
            pltpu.make_async_copy(acc.at[pl.ds(0, sub)], y_hbm.at[pl.ds(0, sub)], ssem).wait()

    @pl.when(f == 0)
    def _():
        @pl.when(i == 0)
        def _():
            acc[0:sub] = jnp.zeros((sub, acc.shape[1]), acc.dtype)

            def body(r, carry):
                start_gather(0, r)
                return carry

            lax.fori_loop(0, cnt * sub, body, 0)

        @pl.when(i <= ni)
        def _():
            scatter_sub(prev_base, 0)

            @pl.when(prev_cnt == 2)
            def _():
                scatter_sub(prev_base, 1)

        @pl.when(i < ni)
        def _():
            gather_wait(cnt)
            xb[0:sub] = xbuf[0:sub].astype(BF16)

            @pl.when(cnt == 2)
            def _():
                xb[sub:2 * sub] = xbuf[sub:2 * sub].astype(BF16)

            @pl.when(i + 1 < ni)
            def _():
                next_base = item_b0_ref[nxt] * sub
                gather_sub(next_base, 0)

                @pl.when(item_cnt_ref[nxt] == 2)
                def _():
                    gather_sub(next_base, 1)

        @pl.when(i == ni)
        def _():
            scatter_wait(prev_cnt)

    @pl.when(i < ni)
    def _():
        wg = wg_ref[...].astype(BF16)
        wu = wu_ref[...].astype(BF16)
        wd = wd_ref[...].astype(BF16)

        def sub_block(hh):
            rows = slice(hh * sub, (hh + 1) * sub)
            x = xb[rows]
            g = _dot(x, wg) + bg_ref[...]
            u = _dot(x, wu) + bu_ref[...]
            g = jnp.minimum(g, SWIGLU_LIMIT)
            u = jnp.clip(u, -SWIGLU_LIMIT, SWIGLU_LIMIT)
            act = g * jax.nn.sigmoid(SWIGLU_ALPHA * g) * (u + 1.0)
            part = _dot(act.astype(BF16), wd)

            @pl.when(f == 0)
            def _():
                if hh == 0:
                    scatter_wait(prev_cnt)
                acc[rows] = part + bd_ref[...]

            @pl.when(f > 0)
            def _():
                acc[rows] += part

        sub_block(0)

        @pl.when(cnt == 2)
        def _():
            sub_block(1)


def _moe_blocks(h2, items, slot_tok, slot_dst, n_rows_out, w_gu, b_gu, w_dn, b_dn, layer):
    item_e, item_b0, item_cnt, nitems = items
    d = h2.shape[1]
    n_exp, _, ff2 = w_gu.shape[1:]
    ff = ff2 // 2
    sub, fc = MOE_TOKENS, _tile(ff, MOE_FF_CHUNK)
    nf = ff // fc
    n_grid_items = item_e.shape[0]

    def fcl(i, f, ni):
        return jnp.where(i < ni[0], f, nf - 1)

    grid_spec = pltpu.PrefetchScalarGridSpec(
        num_scalar_prefetch=6, grid=(n_grid_items, nf),
        in_specs=[
            pl.BlockSpec(memory_space=pl.ANY),
            pl.BlockSpec((None, None, d, fc), lambda i, f, ie, b0, ct, ni, *_: (layer, ie[i], 0, fcl(i, f, ni))),
            pl.BlockSpec((None, None, d, fc), lambda i, f, ie, b0, ct, ni, *_: (layer, ie[i], 0, nf + fcl(i, f, ni))),
            pl.BlockSpec((None, None, 1, fc), lambda i, f, ie, b0, ct, ni, *_: (layer, ie[i], 0, fcl(i, f, ni))),
            pl.BlockSpec((None, None, 1, fc), lambda i, f, ie, b0, ct, ni, *_: (layer, ie[i], 0, nf + fcl(i, f, ni))),
            pl.BlockSpec((None, None, fc, d), lambda i, f, ie, b0, ct, ni, *_: (layer, ie[i], fcl(i, f, ni), 0)),
            pl.BlockSpec((None, None, 1, d), lambda i, f, ie, b0, ct, ni, *_: (layer, ie[i], 0, 0)),
        ],
        out_specs=pl.BlockSpec(memory_space=pl.ANY),
        scratch_shapes=[pltpu.VMEM((MOE_ITEM_SUBS * sub, d), F32), pltpu.VMEM((MOE_ITEM_SUBS * sub, d), BF16),
                        pltpu.VMEM((MOE_ITEM_SUBS * sub, d), F32),
                        pltpu.SemaphoreType.DMA(()), pltpu.SemaphoreType.DMA(())],
    )
    depth = w_gu.shape[0]
    return pl.pallas_call(
        functools.partial(_moe_kernel, sub=sub, nf=nf, n_grid_items=n_grid_items, dump0=n_rows_out - sub),
        grid_spec=grid_spec,
        out_shape=jax.ShapeDtypeStruct((n_rows_out, d), F32),
        compiler_params=_cparams("arbitrary", "arbitrary", vmem_limit=MOE_VMEM_LIMIT),
    )(item_e, item_b0, item_cnt, nitems, slot_tok, slot_dst, h2, w_gu, w_gu, b_gu.reshape(depth, n_exp, 1, ff2),
      b_gu.reshape(depth, n_exp, 1, ff2), w_dn, b_dn.reshape(depth, n_exp, 1, d))


def _moe(h2, logits, w_gu, b_gu, w_dn, b_dn, layer):
    n_tok, d = h2.shape
    n_exp = logits.shape[1]
    sub = MOE_TOKENS
    top_v, top_i = lax.top_k(logits, TOP_K)
    gate_w = jax.nn.softmax(top_v, axis=-1)
    n_asg = n_tok * TOP_K
    flat_e = top_i.reshape(-1)
    onehot = (flat_e[:, None] == jnp.arange(n_exp, dtype=flat_e.dtype)[None, :]).astype(jnp.int32)
    counts = jnp.sum(onehot, axis=0)
    n_sub = (counts + sub - 1) // sub
    pad_ends = jnp.cumsum(n_sub * sub)
    pad_starts = pad_ends - n_sub * sub
    pos = jnp.sum(onehot * (jnp.cumsum(onehot, axis=0) - 1 + pad_starts[None, :]), axis=1).astype(jnp.int32)
    cap = (-(-n_asg // sub) + n_exp) * sub
    slot_asg = jnp.full((cap,), -1, jnp.int32).at[pos].set(jnp.arange(n_asg, dtype=jnp.int32))
    valid = slot_asg >= 0
    slot_tok = jnp.where(valid, slot_asg // TOP_K, 0)
    slot_dst = jnp.where(valid, (slot_asg % TOP_K) * n_tok + slot_asg // TOP_K,
                         n_asg + jnp.arange(cap, dtype=jnp.int32) % sub)
    per_e = (n_sub + MOE_ITEM_SUBS - 1) // MOE_ITEM_SUBS
    item_ends = jnp.cumsum(per_e)
    nitems = item_ends[-1].astype(jnp.int32)
    n_grid_items = (cap // sub + n_exp * (MOE_ITEM_SUBS - 1)) // MOE_ITEM_SUBS + 1
    idx = jnp.arange(n_grid_items, dtype=jnp.int32)
    e_of = jnp.minimum(jnp.sum((item_ends[None, :] <= idx[:, None]).astype(jnp.int32), axis=1), n_exp - 1)
    sel = (e_of[:, None] == jnp.arange(n_exp)[None, :]).astype(jnp.int32)
    first = idx - jnp.sum(sel * (item_ends - per_e)[None, :], axis=1)
    item_b0 = jnp.sum(sel * (pad_starts // sub)[None, :], axis=1) + first * MOE_ITEM_SUBS
    item_cnt = jnp.clip(jnp.sum(sel * n_sub[None, :], axis=1) - first * MOE_ITEM_SUBS, 0, MOE_ITEM_SUBS)
    used = idx < nitems
    last_e = jnp.sum(jnp.where(idx == nitems - 1, e_of, 0))
    items = (jnp.where(used, e_of, last_e).astype(jnp.int32), jnp.where(used, item_b0, 0).astype(jnp.int32),
             jnp.where(used, item_cnt, 0).astype(jnp.int32), nitems.reshape(1))
    y4 = _moe_blocks(h2, items, slot_tok, slot_dst, n_asg + sub, w_gu, b_gu, w_dn, b_dn, layer)
    return y4, gate_w


def _combine_kernel(x_ref, g_ref, w_ref, y0_ref, y1_ref, y2_ref, y3_ref, o_ref):
    w = w_ref[...]
    m = None
    for j, y_ref in enumerate((y0_ref, y1_ref, y2_ref, y3_ref)):
        term = w[:, j:j + 1] * y_ref[...]
        m = term if m is None else m + term
    o_ref[0] = x_ref[0] + g_ref[0] * m


def _moe_combine(x, gate, y4, gate_w, row0):
    bsz, t_len, d = x.shape
    n_tok = gate_w.shape[0]
    tt = _tile(t_len, 256)
    tpb = t_len // tt
    t0 = row0 // tt
    nt = n_tok // tt

    def y_spec(j):
        return pl.BlockSpec((tt, d), lambda b, t: (j * nt + t0 + b * tpb + t, 0))

    if gate.shape[0] == 1:
        g_spec = pl.BlockSpec((1, 1, d), lambda b, t: (0, 0, 0))
    else:
        g_spec = pl.BlockSpec((1, 1, d), lambda b, t: (b, 0, 0))
    return pl.pallas_call(
        _combine_kernel, grid=(bsz, tpb),
        in_specs=[pl.BlockSpec((1, tt, d), lambda b, t: (b, t, 0)), g_spec,
                  pl.BlockSpec((tt, TOP_K), lambda b, t: (t0 + b * tpb + t, 0)),
                  y_spec(0), y_spec(1), y_spec(2), y_spec(3)],
        out_specs=pl.BlockSpec((1, tt, d), lambda b, t: (b, t, 0)),
        out_shape=jax.ShapeDtypeStruct((bsz, t_len, d), F32),
        compiler_params=_cparams("parallel", "parallel"),
    )(x, gate, gate_w, y4, y4, y4, y4)


def _token_mixers(h, seq, lw, layer, lam_init, ctx):
    latent = ctx is not None
    bsz, t_len = seq
    flat = h.shape[:2]
    rope = _rope_tables(t_len, BRANCH_W) if latent else None
    qkv = _qkv_proj(h, lw['w_qkv'], lw['qkv_gains'], rope).reshape(bsz, t_len, -1)
    g2 = _mm(h, lw['w_lat']).reshape(bsz, t_len, -1)
    g3 = _mm(h, lw['w_ssd']).reshape(bsz, t_len, -1)

    if latent:
        ya = _na_latent(qkv, qkv, qkv, 2, ctx['na_k'], ctx['na_v'], layer, lw['rpb'], q_col=0, k_col=1)
    else:
        ya = _attention(qkv, qkv, qkv, 2, heads=NA_HEADS, maps=1, pieces=_na_pieces, dv=HEAD_DIM,
                        scale=HEAD_DIM ** -0.5, q_col=0, k_col=1, qk_width=BRANCH_W)

    diff = (lw['lam_q1'], lw['lam_k1'], lw['lam_q2'], lw['lam_k2'], lw['g_df_sub'], lam_init)
    df_ctx = (ctx['df_k'], ctx['df_v'], layer, 0) if latent else None
    yb = _attention(qkv, qkv, qkv, 5, heads=DF_HEADS, maps=2, pieces=_df_pieces, dv=2 * HEAD_DIM,
                    scale=HEAD_DIM ** -0.5, ctx=df_ctx, diff=diff, q_col=3, k_col=4, qk_width=BRANCH_W)

    consts = _mla_consts()
    qc, kc, vc, ckv, krope = _mla_prep(g2, lw['g_mla_cq'], lw['g_mla_ckv'], lw['w_uq'], lw['w_ukv'],
                                       lw['g_mla_q'], lw['g_mla_k'], consts, rope)
    mla_ctx = None
    if latent:
        kcx, vcx = _mla_ctx(ctx['mla_ckv'], ctx['mla_krope'], layer, lw['w_ukv'], lw['g_mla_k'], consts)
        mla_ctx = (kcx, vcx, 0, 0)
    yc = _attention(qc, kc, vc, 0, heads=MLA_HEADS, maps=1, pieces=_mla_pieces, dv=V_DIM,
                    scale=MLA_QK ** -0.5, ctx=mla_ctx)

    xs, bm, cm, dt = _ssd_prep(g3, lw['conv_w'], lw['conv_b'], lw['dt_bias'])
    y2, s_fin = _ssd_scan(xs, bm, cm, dt, lw['a_log'], ctx['ssd'] if latent else None, layer)
    yd = _ssd_post(y2, xs, g3, lw['d_skip'], lw['g_ssd'])

    ys = tuple(y.reshape(flat + (BRANCH_W,)) for y in (ya, yb, yc, yd))
    merged = _merge(h, lw['w_gate'], ys, lw['w_branch'], layer)
    if latent:
        return merged, None

    def group(n, *head_shape):
        return qkv[..., n * BRANCH_W:(n + 1) * BRANCH_W].reshape((bsz, t_len) + head_shape)

    new = dict(
        na_k=group(1, NA_HEADS, HEAD_DIM), na_v=group(2, NA_HEADS, HEAD_DIM),
        df_k=group(4, DF_HEADS, 2, HEAD_DIM), df_v=group(5, DF_HEADS, 2 * HEAD_DIM),
        mla_ckv=ckv, mla_krope=krope, ssd=s_fin)
    return merged, new


def kernel(x_prompt, x_sample, cache_na_k, cache_na_v, cache_df_k, cache_df_v, cache_mla_ckv, cache_mla_krope, state_ssd, c, c_ctx, w_mod, b_mod, g_norm1, g_norm2, w_in, g_na_q, g_na_k, rpb, g_df_q, g_df_k, lam_q1, lam_k1, lam_q2, lam_k2, g_df_sub, g_mla_cq, g_mla_ckv, w_uq, w_ukv, g_mla_q, g_mla_k, conv_w, conv_b, dt_bias, a_log, d_skip, g_ssd, w_branch, w_o, w_router, b_router, w_gu, b_gu, w_dn, b_dn):
    depth, d, _ = w_in.shape
    dec_b, _, past = cache_na_k.shape[:3]
    n_p = x_prompt.shape[0] * x_prompt.shape[1]
    n_exp = w_router.shape[-1]
    xp, xs = x_prompt.reshape(1, n_p, d), x_sample

    n_rows = -(-(dec_b + 1) // 8) * 8
    cc = jnp.zeros((n_rows, d), F32).at[:dec_b].set(c).at[dec_b].set(c_ctx)
    mods = _mod_params(cc, w_mod, b_mod)

    ctx_all = dict(
        na_k=cache_na_k.reshape(dec_b, depth, past, NA_HEADS * HEAD_DIM),
        na_v=cache_na_v.reshape(dec_b, depth, past, NA_HEADS * HEAD_DIM),
        df_k=cache_df_k.reshape(dec_b, depth, past, DF_HEADS * 2 * HEAD_DIM),
        df_v=cache_df_v.reshape(dec_b, depth, past, DF_HEADS * 2 * HEAD_DIM),
        mla_ckv=cache_mla_ckv, mla_krope=cache_mla_krope, ssd=state_ssd)

    o_lat = 6 * BRANCH_W
    o_ssd = o_lat + Q_RANK + KV_RANK + ROPE_DIM
    o_gate = o_ssd + BRANCH_W + conv_w.shape[-1] + 2 * SSD_HEADS
    ne_pad = -(-n_exp // LANE) * LANE

    names = ('na_k', 'na_v', 'df_k', 'df_v', 'mla_ckv', 'mla_krope', 'ssd')
    new = {n: [] for n in names}
    for l in range(depth):
        lam_init = 0.8 - 0.6 * math.exp(-0.3 * l)
        wq, wkv, gq, gk = _mla_perm_weights(w_uq[l], w_ukv[l], g_mla_q[l], g_mla_k[l])
        w_l = w_in[l]
        lw = dict(
            w_qkv=w_l[:, :o_lat].astype(BF16), w_lat=w_l[:, o_lat:o_ssd].astype(BF16),
            w_ssd=w_l[:, o_ssd:o_gate].astype(BF16),
            w_gate=w_l[:, o_gate:].astype(BF16),
            qkv_gains=jnp.stack([jnp.tile(g, BRANCH_W // HEAD_DIM) for g in (
                g_na_q[l], g_na_k[l], jnp.ones_like(g_na_q[l]), g_df_q[l], g_df_k[l], jnp.ones_like(g_df_q[l])
            )]).reshape(QKV_GROUPS, 1, BRANCH_W),
            rpb=rpb[l], lam_q1=lam_q1[l], lam_k1=lam_k1[l], lam_q2=lam_q2[l], lam_k2=lam_k2[l], g_df_sub=g_df_sub[l],
            g_mla_cq=g_mla_cq[l], g_mla_ckv=g_mla_ckv[l], w_uq=wq, w_ukv=wkv, g_mla_q=gq, g_mla_k=gk,
            conv_w=conv_w[l], conv_b=conv_b[l], dt_bias=dt_bias[l], a_log=a_log[l], d_skip=d_skip[l],
            g_ssd=g_ssd[l], w_branch=w_branch)
        mc = [m[:, None, :] for m in jnp.split(mods[l, :dec_b], 6, axis=-1)]
        mx = [m[:, None, :] for m in jnp.split(mods[l, dec_b:dec_b + 1], 6, axis=-1)]

        hp = _norm_mod(xp, g_norm1[l], mx[0], mx[1])
        mp, ctx_new = _token_mixers(hp, x_prompt.shape[:2], lw, l, lam_init, None)
        xp = _mm_residual(mp, w_o, l, xp, mx[2])
        hs = _norm_mod(xs, g_norm1[l], mc[0], mc[1])
        ms, _ = _token_mixers(hs, x_sample.shape[:2], lw, l, lam_init, ctx_all)
        xs = _mm_residual(ms, w_o, l, xs, mc[2])

        wr = jnp.zeros((d, ne_pad), F32).at[:, :n_exp].set(w_router[l])
        br = jnp.zeros((1, ne_pad), F32).at[0, :n_exp].set(b_router[l])
        hp2, lg_p = _norm_mod(xp, g_norm2[l], mx[3], mx[4], router=(wr, br))
        hs2, lg_s = _norm_mod(xs, g_norm2[l], mc[3], mc[4], router=(wr, br))
        h2 = jnp.concatenate([hp2.reshape(-1, d), hs2.reshape(-1, d)], axis=0)
        logits = jnp.concatenate([lg_p.reshape(-1, ne_pad), lg_s.reshape(-1, ne_pad)], axis=0)[:, :n_exp]
        y4, gate_w = _moe(h2, logits, w_gu, b_gu, w_dn, b_dn, l)
        xp = _moe_combine(xp, mx[5], y4, gate_w, 0)
        xs = _moe_combine(xs, mc[5], y4, gate_w, n_p)
        for n in names:
            new[n].append(ctx_new[n])
    return (xp.reshape(x_prompt.shape), xs) + tuple(jnp.stack(new[n], axis=1) for n in names)
```
